```python
import math, functools
import jax, jax.numpy as jnp
from jax import lax
import numpy as np

D_MODEL = 1024
BATCH = 8
SEQ = 2048
DEPTH = 4
DEC_BATCH = 128
DEC_SEQ = 8
PAST_LEN = 2048
PAGE_SIZE = 128

N_A_LAYERS = DEPTH // 2
N_B_LAYERS = DEPTH - N_A_LAYERS
EPS = 1e-6

A_HEADS = 8
A_DK = 128
A_DV = 128
CONV_W = 4
CHUNK = 64
A_QK_DIM = A_HEADS * A_DK
A_V_DIM = A_HEADS * A_DV
A_CONV_DIM = 2 * A_QK_DIM + A_V_DIM
A_PROJ_DIM = A_CONV_DIM + A_V_DIM + 2 * A_HEADS

B_GROUPS = ((128, 1), (512, 4), (2048, 16))
N_GROUPS = 3
B_HEADS = 16
B_DH = 64
B_KV_HEADS = 4
B_QPK = B_HEADS // B_KV_HEADS
B_BLOCK = 128
MAX_WINDOW = 2048

D_FF = -(-8 * D_MODEL // (3 * 256)) * 256

kernel_name = 'yoco_gated_deltanet_dilated_window_step'


def rms_norm(x, gain):
    xf = x.astype(jnp.float32)
    y = xf * lax.rsqrt(jnp.mean(xf * xf, axis=-1, keepdims=True) + EPS)
    return (y * gain.astype(jnp.float32)).astype(x.dtype)


def l2_normalize(x):
    return x * lax.rsqrt(jnp.sum(x * x, axis=-1, keepdims=True) + EPS)


def swiglu(x, w_in, w_out):
    gate, up = jnp.split(x @ w_in, 2, axis=-1)
    return (jax.nn.silu(gate) * up) @ w_out


def causal_depthwise_conv(xh, w):
    c = xh.shape[-1]
    return lax.conv_general_dilated(xh, w[:, None, :].astype(xh.dtype), window_strides=(1,), padding='VALID',
                                    dimension_numbers=('NWC', 'WIO', 'NWC'), feature_group_count=c)


def gated_delta_rule(q, k, v, g, beta, s0):
    b, t, h, dk = q.shape
    n = -(-t // CHUNK)
    pad = n * CHUNK - t

    def blocks(z):
        z = jnp.pad(z, [(0, 0), (0, pad)] + [(0, 0)] * (z.ndim - 2))
        z = z.reshape((b, n, CHUNK) + z.shape[2:])
        return jnp.transpose(z, (1, 0, 3, 2) + tuple(range(4, z.ndim)))

    q, k, v, g, beta = blocks(q), blocks(k), blocks(v), blocks(g), blocks(beta)
    gcum = jnp.cumsum(g, axis=-1)
    idx = np.arange(CHUNK)
    incl = idx[:, None] >= idx[None, :]
    strict = idx[:, None] > idx[None, :]
    decay = jnp.exp(jnp.where(incl, gcum[..., :, None] - gcum[..., None, :], -jnp.inf))
    a_mat = jnp.where(strict, beta[..., :, None] * decay * jnp.einsum('nbhtd,nbhsd->nbhts', k, k), 0.0)
    gam = jnp.exp(gcum)
    rhs = jnp.concatenate([(beta * gam)[..., None] * k, beta[..., None] * v], axis=-1)
    sol = lax.linalg.triangular_solve(jnp.eye(CHUNK, dtype=jnp.float32) + a_mat, rhs,
                                      left_side=True, lower=True, unit_diagonal=True)
    w_mat, u_base = sol[..., :dk], sol[..., dk:]
    attn = decay * jnp.einsum('nbhtd,nbhsd->nbhts', q, k)
    q_dec = gam[..., None] * q
    k_dec = jnp.exp(gcum[..., -1:] - gcum)[..., None] * k
    g_end = jnp.exp(gcum[..., -1])

    def step(s, xs):
        w_c, u_c, attn_c, q_c, k_c, ge_c = xs
        u = u_c - jnp.einsum('bhcd,bhde->bhce', w_c, s)
        o = jnp.einsum('bhcd,bhde->bhce', q_c, s) + jnp.einsum('bhcs,bhse->bhce', attn_c, u)
        s = ge_c[..., None, None] * s + jnp.einsum('bhcd,bhce->bhde', k_c, u)
        return s, o

    s_fin, o = lax.scan(step, s0, (w_mat, u_base, attn, q_dec, k_dec, g_end))
    o = jnp.transpose(o, (1, 0, 3, 2, 4)).reshape(b, n * CHUNK, h, -1)[:, :t]
    return o, s_fin


def gdn_mixer(h, conv_hist, s0, w_in, conv_w, a_log, dt_bias, o_gain, w_out):
    b, l, _ = h.shape
    f32 = jnp.float32
    proj = h @ w_in
    qkv = proj[..., :A_CONV_DIM]
    gate = proj[..., A_CONV_DIM:A_CONV_DIM + A_V_DIM]
    beta_raw = proj[..., A_CONV_DIM + A_V_DIM:A_CONV_DIM + A_V_DIM + A_HEADS]
    decay_raw = proj[..., A_CONV_DIM + A_V_DIM + A_HEADS:]
    xh = jnp.concatenate([conv_hist.astype(qkv.dtype), qkv], axis=1)
    new_hist = xh[:, -(CONV_W - 1):]
    c = jax.nn.silu(causal_depthwise_conv(xh, conv_w)).astype(f32)
    q = l2_normalize(c[..., :A_QK_DIM].reshape(b, l, A_HEADS, A_DK)) * (A_DK ** -0.5)
    k = l2_normalize(c[..., A_QK_DIM:2 * A_QK_DIM].reshape(b, l, A_HEADS, A_DK))
    v = c[..., 2 * A_QK_DIM:].reshape(b, l, A_HEADS, A_DV)
    beta = jax.nn.sigmoid(beta_raw.astype(f32))
    g = -jnp.exp(a_log.astype(f32)) * jax.nn.softplus(decay_raw.astype(f32) + dt_bias.astype(f32))
    o, s_new = gated_delta_rule(q, k, v, g, beta, s0.astype(f32))
    o = rms_norm(o, o_gain) * jax.nn.silu(gate.astype(f32).reshape(b, l, A_HEADS, A_DV))
    y = o.reshape(b, l, A_V_DIM).astype(h.dtype) @ w_out
    return y, s_new.astype(s0.dtype), new_hist.astype(conv_hist.dtype)


def alibi_slopes():
    n = N_GROUPS * B_HEADS
    return (2.0 ** (-8.0 * np.arange(1, n + 1) / n)).astype(np.float32).reshape(N_GROUPS, B_HEADS)


def dilated_band_attention(q, win, dil, slope, *, k, v):
    b, s, h, dh = q.shape
    nkv = k.shape[2]
    qpk = h // nkv
    span = win // dil
    n_dec = s // dil
    nb = -(-n_dec // B_BLOCK)
    lp = nb * B_BLOCK

    def residues(z):
        z = jnp.swapaxes(z.reshape((b, n_dec, dil) + z.shape[2:]), 1, 2)
        return jnp.pad(z, [(0, 0), (0, 0), (0, lp - n_dec)] + [(0, 0)] * (z.ndim - 3))

    def key_windows(z):
        zp = jnp.pad(residues(z), [(0, 0), (0, 0), (B_BLOCK, 0), (0, 0), (0, 0)])
        prev = zp[:, :, :lp].reshape(b, dil, nb, B_BLOCK, nkv, dh)
        cur = zp[:, :, B_BLOCK:].reshape(b, dil, nb, B_BLOCK, nkv, dh)
        return jnp.concatenate([prev, cur], axis=3)

    qr = residues(q).reshape(b, dil, nb, B_BLOCK, nkv, qpk, dh)
    kw = key_windows(k)
    vw = key_windows(v)
    qi = np.arange(B_BLOCK)[:, None]
    kj = np.arange(2 * B_BLOCK)[None, :]
    delta = B_BLOCK + qi - kj
    valid = (delta >= 0) & (delta <= span) & ((np.arange(nb)[:, None, None] > 0) | (kj >= B_BLOCK))
    bias = -slope[:, :, None, None] * (dil * delta).astype(np.float32)
    scores = jnp.einsum('brnqgpd,brnkgd->brngpqk', qr, kw, preferred_element_type=jnp.float32) * (dh ** -0.5) + bias
    scores = jnp.where(valid[:, None, None], scores, -jnp.inf)
    lse = jax.nn.logsumexp(scores, axis=-1)
    probs = jnp.exp(scores - lse[..., None])
    out = jnp.einsum('brngpqk,brnkgd->brnqgpd', probs, vw.astype(jnp.float32))
    out = jnp.swapaxes(out.reshape(b, dil, lp, h, dh)[:, :, :n_dec], 1, 2).reshape(b, s, h, dh)
    lse = jnp.transpose(lse, (0, 1, 2, 5, 3, 4)).reshape(b, dil, lp, h)[:, :, :n_dec]
    lse = jnp.swapaxes(lse, 1, 2).reshape(b, s, h)
    return out, lse


def dilated_gather_attention(q, win, dil, slope, *, k_all, v_all):
    b, l, h, dh = q.shape
    t, nkv = k_all.shape[1], k_all.shape[2]
    qpk = h // nkv
    dist = dil * np.arange(win // dil + 1)
    pos = (t - l) + np.arange(l)[:, None] - dist[None, :]
    valid = pos >= 0
    idx = np.maximum(pos, 0)
    kg = k_all[:, idx]
    vg = v_all[:, idx]
    qr = q.reshape(b, l, nkv, qpk, dh)
    bias = -slope[:, :, None] * dist.astype(np.float32)
    scores = jnp.einsum('blgpd,bltgd->blgpt', qr, kg, preferred_element_type=jnp.float32) * (dh ** -0.5) + bias
    scores = jnp.where(valid[:, None, None, :], scores, -jnp.inf)
    lse = jax.nn.logsumexp(scores, axis=-1)
    probs = jnp.exp(scores - lse[..., None])
    out = jnp.einsum('blgpt,bltgd->blgpd', probs, vg.astype(jnp.float32)).reshape(b, l, h, dh)
    return out, lse.reshape(b, l, h)


def dilated_mixture(h, w_q, w_o, attend):
    b, l, _ = h.shape
    q = (h @ w_q).reshape(b, l, N_GROUPS, B_HEADS, B_DH)
    slopes = alibi_slopes()
    outs, lses = [], []
    for gi, (win, dil) in enumerate(B_GROUPS):
        o, s = attend(q[:, :, gi], win, dil, slopes[gi].reshape(B_KV_HEADS, B_QPK))
        outs.append(o)
        lses.append(s)
    weights = jax.nn.softmax(jnp.stack(lses, axis=0), axis=0)
    o = jnp.einsum('gblh,gblhd->blhd', weights, jnp.stack(outs, axis=0))
    return o.reshape(b, l, B_HEADS * B_DH).astype(h.dtype) @ w_o


def decoder_trunk(x, conv_hist, delta_s, kv_past, norms, kv_norm, a_w_in, a_conv_w, a_log, a_dt_bias,
                  a_o_gain, a_w_out, b_w_kv, b_w_q, b_w_o, ffn_w_in, ffn_w_out):
    new_hist, new_delta = [], []
    attend = None
    k_new = v_new = None
    for layer in range(DEPTH):
        if layer < N_A_LAYERS:
            y, s, hst = gdn_mixer(rms_norm(x, norms[layer, 0]), conv_hist[layer], delta_s[layer],
                                  a_w_in[layer], a_conv_w[layer], a_log[layer], a_dt_bias[layer],
                                  a_o_gain[layer], a_w_out[layer])
            new_hist.append(hst)
            new_delta.append(s)
        else:
            if layer == N_A_LAYERS:
                b, l, _ = x.shape
                kv = (rms_norm(x, kv_norm) @ b_w_kv).reshape(b, l, 2, B_KV_HEADS, B_DH)
                k_new, v_new = kv[:, :, 0], kv[:, :, 1]
                if kv_past is None:
                    attend = functools.partial(dilated_band_attention, k=k_new, v=v_new)
                else:
                    k_all = jnp.concatenate([kv_past[0].astype(k_new.dtype), k_new], axis=1)
                    v_all = jnp.concatenate([kv_past[1].astype(v_new.dtype), v_new], axis=1)
                    attend = functools.partial(dilated_gather_attention, k_all=k_all, v_all=v_all)
            j = layer - N_A_LAYERS
            y = dilated_mixture(rms_norm(x, norms[layer, 0]), b_w_q[j], b_w_o[j], attend)
        x = x + rms_norm(y, norms[layer, 1])
        x = x + rms_norm(swiglu(rms_norm(x, norms[layer, 2]), ffn_w_in[layer], ffn_w_out[layer]), norms[layer, 3])
    return x, jnp.stack(new_hist, axis=0), jnp.stack(new_delta, axis=0), k_new, v_new


def setup_inputs(seed: int = 0) -> dict:
    key = jax.random.key(seed)
    ks = jax.random.split(key, 20)
    f32 = jnp.float32
    win_buf = min(MAX_WINDOW, PAST_LEN)

    def nrm(k, shape, scale):
        return jax.random.normal(k, shape, f32) * scale

    dt = jnp.exp(jax.random.uniform(ks[11], (N_A_LAYERS, A_HEADS), f32, math.log(1e-3), math.log(1e-1)))
    return {
        'x_prompt': nrm(ks[0], (BATCH, SEQ, D_MODEL), 1.0),
        'x_sample': nrm(ks[1], (DEC_BATCH, DEC_SEQ, D_MODEL), 1.0),
        'state_conv': nrm(ks[2], (N_A_LAYERS, DEC_BATCH, CONV_W - 1, A_CONV_DIM), 1.0),
        'state_delta': nrm(ks[3], (N_A_LAYERS, DEC_BATCH, A_HEADS, A_DK, A_DV), 0.3),
        'cache_k': nrm(ks[4], (DEC_BATCH, win_buf, B_KV_HEADS, B_DH), 1.0),
        'cache_v': nrm(ks[5], (DEC_BATCH, win_buf, B_KV_HEADS, B_DH), 1.0),
        'norms': 1.0 + nrm(ks[6], (DEPTH, 4, D_MODEL), 0.05),
        'kv_norm': 1.0 + nrm(ks[7], (D_MODEL,), 0.05),
        'a_w_in': nrm(ks[8], (N_A_LAYERS, D_MODEL, A_PROJ_DIM), D_MODEL ** -0.5),
        'a_conv_w': nrm(ks[9], (N_A_LAYERS, CONV_W, A_CONV_DIM), CONV_W ** -0.5),
        'a_log': jnp.log(jax.random.uniform(ks[10], (N_A_LAYERS, A_HEADS), f32, 1.0, 16.0)),
        'a_dt_bias': dt + jnp.log(-jnp.expm1(-dt)),
        'a_o_gain': 1.0 + nrm(ks[12], (N_A_LAYERS, A_DV), 0.05),
        'a_w_out': nrm(ks[13], (N_A_LAYERS, A_V_DIM, D_MODEL), A_V_DIM ** -0.5),
        'b_w_kv': nrm(ks[14], (D_MODEL, 2 * B_KV_HEADS * B_DH), D_MODEL ** -0.5),
        'b_w_q': nrm(ks[15], (N_B_LAYERS, D_MODEL, N_GROUPS * B_HEADS * B_DH), D_MODEL ** -0.5),
        'b_w_o': nrm(ks[16], (N_B_LAYERS, B_HEADS * B_DH, D_MODEL), (B_HEADS * B_DH) ** -0.5),
        'ffn_w_in': nrm(ks[17], (DEPTH, D_MODEL, 2 * D_FF), D_MODEL ** -0.5),
        'ffn_w_out': nrm(ks[18], (DEPTH, D_FF, D_MODEL), D_FF ** -0.5),
    }


def reference(x_prompt, x_sample, state_conv, state_delta, cache_k, cache_v, norms, kv_norm, a_w_in,
              a_conv_w, a_log, a_dt_bias, a_o_gain, a_w_out, b_w_kv, b_w_q, b_w_o, ffn_w_in, ffn_w_out):
    bp, sp = x_prompt.shape[0], x_prompt.shape[1]
    zero_hist = jnp.zeros((N_A_LAYERS, bp, CONV_W - 1, A_CONV_DIM), x_prompt.dtype)
    zero_delta = jnp.zeros((N_A_LAYERS, bp, A_HEADS, A_DK, A_DV), x_prompt.dtype)
    y_prompt, conv_p, delta_p, k_p, v_p = decoder_trunk(
        x_prompt, zero_hist, zero_delta, None, norms, kv_norm, a_w_in, a_conv_w, a_log, a_dt_bias,
        a_o_gain, a_w_out, b_w_kv, b_w_q, b_w_o, ffn_w_in, ffn_w_out)
    win_p = min(MAX_WINDOW, sp)
    cache_k_prompt = k_p[:, sp - win_p:]
    cache_v_prompt = v_p[:, sp - win_p:]
    y_sample, conv_s, delta_s, k_s, v_s = decoder_trunk(
        x_sample, state_conv, state_delta, (cache_k, cache_v), norms, kv_norm, a_w_in, a_conv_w, a_log,
        a_dt_bias, a_o_gain, a_w_out, b_w_kv, b_w_q, b_w_o, ffn_w_in, ffn_w_out)
    return (y_prompt, y_sample, conv_p, delta_p, cache_k_prompt, cache_v_prompt, conv_s, delta_s, k_s, v_s)
```

```python
import functools
import math

import jax
import jax.numpy as jnp
import numpy as np
from jax import lax
from jax.experimental import pallas as pl
from jax.experimental.pallas import tpu as pltpu

F32 = jnp.float32
BF16 = jnp.bfloat16

D_MODEL = 1024
DEPTH = 4
N_A_LAYERS = 2
EPS = 1e-6

A_HEADS = 8
A_DK = 128
A_DV = 128
CONV_W = 4
A_QK_DIM = A_HEADS * A_DK
A_V_DIM = A_HEADS * A_DV
A_CONV_DIM = 2 * A_QK_DIM + A_V_DIM
A_MAIN_DIM = A_CONV_DIM + A_V_DIM
GATE_PAD = 128

B_GROUPS = ((128, 1), (512, 4), (2048, 16))
N_GROUPS = 3
B_HEADS = 16
B_DH = 64
B_KV_HEADS = 4
B_QPK = B_HEADS // B_KV_HEADS
B_BLOCK = 128
B_KV_DIM = B_KV_HEADS * B_DH
B_Q_DIM = B_HEADS * B_DH
MAX_WINDOW = 2048

D_FF = 2816
FF_CHUNK = 256

NEG = -1e30
V7X_VMEM_LIMIT = 56 * 1024 * 1024
HIGHEST = lax.Precision.HIGHEST


def _params(*sem):
    return pltpu.CompilerParams(dimension_semantics=sem, vmem_limit_bytes=V7X_VMEM_LIMIT)


def _resident(shape):
    nd = len(shape)
    return pl.BlockSpec(shape, lambda *_: (0,) * nd, pipeline_mode=pl.Buffered(1))


def _rms(x):
    return x * lax.rsqrt(jnp.mean(x * x, axis=-1, keepdims=True) + EPS)


def _silu(x):
    return x * jax.nn.sigmoid(x)


def _norm_proj_kernel(x_ref, g_ref, *refs, group_of, outs, n_chunk):
    nw = len(group_of)
    w_refs, o_refs = refs[:nw], refs[nw:]
    xs = _rms(x_ref[...])
    normed = {}
    for wi, grp in enumerate(group_of):
        if grp not in normed:
            normed[grp] = (xs * g_ref[grp:grp + 1, :]).astype(BF16)
        xn = normed[grp]
        n = w_refs[wi].shape[1]
        for n0 in range(0, n, n_chunk):
            n1 = min(n0 + n_chunk, n)
            res = jnp.dot(xn, w_refs[wi][:, n0:n1], preferred_element_type=F32)
            for oi, (src, _) in enumerate(outs):
                if src == wi:
                    o_refs[oi][:, n0:n1] = res.astype(o_refs[oi].dtype)


def _norm_proj(x, gains, weights, group_of, outs, tm):
    m, d = x.shape
    tm = min(tm, m)
    kern = functools.partial(_norm_proj_kernel, group_of=tuple(group_of), outs=tuple(outs), n_chunk=512)
    return pl.pallas_call(
        kern,
        grid=(m // tm,),
        in_specs=[pl.BlockSpec((tm, d), lambda i: (i, 0)), _resident(gains.shape)]
        + [_resident(w.shape) for w in weights],
        out_specs=[pl.BlockSpec((tm, weights[src].shape[1]), lambda i: (i, 0)) for src, _ in outs],
        out_shape=[jax.ShapeDtypeStruct((m, weights[src].shape[1]), dt) for src, dt in outs],
        compiler_params=_params("parallel"),
        name="norm_proj",
    )(x, gains, *weights)


def _gdn_kernel(pm_ref, bd_ref, hist_ref, s0_ref, cw_ref, ab_ref, og_ref,
                o_ref, s_ref, hnew_ref, xh_ref, *, bt, chunk, mm_dtype):
    nh, c = A_HEADS, chunk
    bh = bt * nh

    @pl.when(pl.program_id(1) == 0)
    def _start():
        s_ref[...] = s0_ref[...]
        xh_ref[:, 5:8, :] = hist_ref[...]

    x = pm_ref[:, :, 0:A_CONV_DIM]
    xh_ref[:, 8:8 + c, :] = x
    cw = cw_ref[...]
    conv = x * cw[CONV_W - 1:CONV_W, :]
    for j in range(1, CONV_W):
        conv = conv + xh_ref[:, 8 - j:8 - j + c, :] * cw[CONV_W - 1 - j:CONV_W - j, :]
    tail = xh_ref[:, 5 + c:8 + c, :]
    xh_ref[:, 5:8, :] = tail
    hnew_ref[...] = tail
    act = _silu(conv)

    def heads(base):
        return jnp.stack([act[bi, :, base + h * A_DK: base + (h + 1) * A_DK]
                          for bi in range(bt) for h in range(nh)])

    qr, kr, v = heads(0), heads(A_QK_DIM), heads(2 * A_QK_DIM)
    q = qr * (lax.rsqrt(jnp.sum(qr * qr, axis=-1, keepdims=True) + EPS) * (A_DK ** -0.5))
    k = kr * lax.rsqrt(jnp.sum(kr * kr, axis=-1, keepdims=True) + EPS)

    bd = bd_ref[...]
    beta_all = jax.nn.sigmoid(bd)
    z = bd + ab_ref[1:2, :]
    softplus = jnp.maximum(z, 0.0) + jnp.log1p(jnp.exp(-jnp.abs(z)))
    g_all = -jnp.exp(ab_ref[0:1, :]) * softplus

    ti = lax.broadcasted_iota(jnp.int32, (c, c), 0)
    si = lax.broadcasted_iota(jnp.int32, (c, c), 1)
    incl = ti >= si
    strict = ti > si
    tri = incl.astype(F32)
    gcol, grow, bcol, glast = [], [], [], []
    for bi in range(bt):
        gc = jnp.dot(tri, g_all[bi], precision=HIGHEST, preferred_element_type=F32)
        gct = gc.T
        for h in range(nh):
            gcol.append(gc[:, nh + h:nh + h + 1])
            grow.append(gct[nh + h:nh + h + 1, :])
            glast.append(gc[c - 1:c, nh + h:nh + h + 1])
            bcol.append(beta_all[bi][:, h:h + 1])
    gcol, grow, bcol, glast = jnp.stack(gcol), jnp.stack(grow), jnp.stack(bcol), jnp.stack(glast)

    decay = jnp.exp(jnp.where(incl, gcol - grow, NEG))
    gam = jnp.exp(gcol)

    def bmm(a, b):
        return jnp.einsum("bij,bjk->bik", a.astype(mm_dtype), b.astype(mm_dtype),
                          preferred_element_type=F32)

    def bmm_nt(a, b):
        return jnp.einsum("bid,bjd->bij", a.astype(mm_dtype), b.astype(mm_dtype),
                          preferred_element_type=F32)

    kk = bmm_nt(k, k)
    qk = bmm_nt(q, k)
    m_pow = jnp.where(strict, -(bcol * decay * kk), 0.0)
    q_inv = m_pow
    for _ in range(int(math.log2(c)) - 1):
        m_pow = bmm(m_pow, m_pow)
        q_inv = q_inv + m_pow + bmm(q_inv, m_pow)
    rhs = jnp.concatenate([(bcol * gam) * k, bcol * v], axis=-1)
    sol = rhs + bmm(q_inv, rhs)
    w_mat, u_base = sol[..., :A_DK], sol[..., A_DK:]

    s_old = s_ref[...].reshape(bh, A_DK, A_DV)
    u = u_base - bmm(w_mat, s_old)
    o = bmm(gam * q, s_old) + bmm(decay * qk, u)
    k_dec = jnp.exp(glast - gcol) * k
    s_new = jnp.exp(glast) * s_old + bmm(jnp.swapaxes(k_dec, 1, 2), u)
    s_ref[...] = s_new.reshape(bt, nh, A_DK, A_DV)

    on = _rms(o) * og_ref[...]
    for bi in range(bt):
        for h in range(nh):
            gate = pm_ref[bi, :, A_CONV_DIM + h * A_DV:A_CONV_DIM + (h + 1) * A_DV]
            o_ref[bi, :, h * A_DV:(h + 1) * A_DV] = (on[bi * nh + h] * _silu(gate)).astype(o_ref.dtype)


def _gdn(pm, bd, hist, s0, conv_w, ab, o_gain, bt, chunk, mm_dtype):
    b, t, _ = pm.shape
    kern = functools.partial(_gdn_kernel, bt=bt, chunk=chunk, mm_dtype=mm_dtype)
    return pl.pallas_call(
        kern,
        grid=(b // bt, t // chunk),
        in_specs=[
            pl.BlockSpec((bt, chunk, A_MAIN_DIM), lambda i, j: (i, j, 0)),
            pl.BlockSpec((bt, chunk, GATE_PAD), lambda i, j: (i, j, 0)),
            pl.BlockSpec((bt, CONV_W - 1, A_CONV_DIM), lambda i, j: (i, 0, 0)),
            pl.BlockSpec((bt, A_HEADS, A_DK, A_DV), lambda i, j: (i, 0, 0, 0)),
            pl.BlockSpec((CONV_W, A_CONV_DIM), lambda i, j: (0, 0)),
            pl.BlockSpec((2, GATE_PAD), lambda i, j: (0, 0)),
            pl.BlockSpec((1, A_DV), lambda i, j: (0, 0)),
        ],
        out_specs=[
            pl.BlockSpec((bt, chunk, A_V_DIM), lambda i, j: (i, j, 0)),
            pl.BlockSpec((bt, A_HEADS, A_DK, A_DV), lambda i, j: (i, 0, 0, 0)),
            pl.BlockSpec((bt, CONV_W - 1, A_CONV_DIM), lambda i, j: (i, 0, 0)),
        ],
        out_shape=[
            jax.ShapeDtypeStruct((b, t, A_V_DIM), BF16),
            jax.ShapeDtypeStruct((b, A_HEADS, A_DK, A_DV), F32),
            jax.ShapeDtypeStruct((b, CONV_W - 1, A_CONV_DIM), F32),
        ],
        scratch_shapes=[pltpu.VMEM((bt, chunk + 8, A_CONV_DIM), F32)],
        compiler_params=_params("parallel", "arbitrary"),
        name="gdn_mixer",
    )(pm, bd, hist, s0, conv_w, ab, o_gain)


def _post_ffn_kernel(x_ref, y_ref, wo_ref, g_ref, wi_ref, w2_ref, out_ref, acc_ref):
    y = jnp.dot(y_ref[...], wo_ref[...], preferred_element_type=F32)
    x1 = x_ref[...] + _rms(y) * g_ref[0:1, :]
    out_ref[...] = x1
    hn = (_rms(x1) * g_ref[1:2, :]).astype(BF16)
    for j in range(0, D_FF, FF_CHUNK):
        gate = jnp.dot(hn, wi_ref[:, j:j + FF_CHUNK], preferred_element_type=F32)
        up = jnp.dot(hn, wi_ref[:, D_FF + j:D_FF + j + FF_CHUNK], preferred_element_type=F32)
        part = jnp.dot((_silu(gate) * up).astype(BF16), w2_ref[j:j + FF_CHUNK, :],
                       preferred_element_type=F32)
        if j == 0:
            acc_ref[...] = part
        else:
            acc_ref[...] += part
    out_ref[...] += _rms(acc_ref[...]) * g_ref[2:3, :]


def _post_ffn(x, y, w_o, gains, w_in, w_out, tm):
    m, d = x.shape
    tm = min(tm, m)
    return pl.pallas_call(
        _post_ffn_kernel,
        grid=(m // tm,),
        in_specs=[
            pl.BlockSpec((tm, d), lambda i: (i, 0)),
            pl.BlockSpec((tm, y.shape[1]), lambda i: (i, 0)),
            _resident(w_o.shape), _resident(gains.shape), _resident(w_in.shape), _resident(w_out.shape),
        ],
        out_specs=pl.BlockSpec((tm, d), lambda i: (i, 0)),
        out_shape=jax.ShapeDtypeStruct((m, d), F32),
        scratch_shapes=[pltpu.VMEM((tm, d), F32)],
        compiler_params=_params("parallel"),
        name="post_ffn",
    )(x, y, w_o, gains, w_in, w_out)


def _alibi_slopes():
    n = N_GROUPS * B_HEADS
    return (2.0 ** (-8.0 * np.arange(1, n + 1) / n)).astype(np.float32).reshape(N_GROUPS, B_HEADS)


def _band_attn_kernel(q_ref, kp_ref, kc_ref, vp_ref, vc_ref, o_ref, lse_ref, *, dil, span, slopes):
    first = pl.program_id(2) == 0
    qi = lax.broadcasted_iota(jnp.int32, (B_BLOCK, 2 * B_BLOCK), 0)
    kj = lax.broadcasted_iota(jnp.int32, (B_BLOCK, 2 * B_BLOCK), 1)
    delta = B_BLOCK + qi - kj
    valid = (delta >= 0) & (delta <= span) & ((kj >= B_BLOCK) | jnp.logical_not(first))
    dist = (dil * delta).astype(F32)
    q = q_ref[0]
    kp, kc, vp, vc = kp_ref[0], kc_ref[0], vp_ref[0], vc_ref[0]
    for g in range(B_KV_HEADS):
        cols = slice(g * B_DH, (g + 1) * B_DH)
        kg = jnp.concatenate([kp[:, cols], kc[:, cols]], axis=0)
        vg = jnp.concatenate([vp[:, cols], vc[:, cols]], axis=0)
        qg = jnp.concatenate([q[:, (g * B_QPK + p) * B_DH:(g * B_QPK + p + 1) * B_DH]
                              for p in range(B_QPK)], axis=0)
        s = lax.dot_general(qg, kg, (((1,), (1,)), ((), ())), preferred_element_type=F32)
        es, ls = [], []
        for p in range(B_QPK):
            h = g * B_QPK + p
            sp = s[p * B_BLOCK:(p + 1) * B_BLOCK] - float(slopes[h]) * dist
            sp = jnp.where(valid, sp, NEG)
            mx = jnp.max(sp, axis=-1, keepdims=True)
            e = jnp.exp(sp - mx)
            l = jnp.sum(e, axis=-1, keepdims=True)
            lse_ref[0, 0, :, h:h + 1] = mx + jnp.log(l)
            es.append(e.astype(BF16))
            ls.append(l)
        pv = jnp.dot(jnp.concatenate(es, axis=0), vg, preferred_element_type=F32)
        for p in range(B_QPK):
            h = g * B_QPK + p
            o_ref[0, :, h * B_DH:(h + 1) * B_DH] = (pv[p * B_BLOCK:(p + 1) * B_BLOCK] / ls[p]).astype(o_ref.dtype)


def _band_attn(q, kb, vb, gi, b, s):
    win, dil = B_GROUPS[gi]
    n_dec = s // dil
    nb = n_dec // B_BLOCK
    qv = q.reshape(b, n_dec, dil * N_GROUPS * B_Q_DIM)
    kv = kb.reshape(b, n_dec, dil * B_KV_DIM)
    vv = vb.reshape(b, n_dec, dil * B_KV_DIM)
    cur = lambda bi, r, j: (bi, j, r)
    prev = lambda bi, r, j: (bi, jnp.maximum(j - 1, 0), r)
    kern = functools.partial(_band_attn_kernel, dil=dil, span=win // dil,
                             slopes=tuple(float(v) for v in _alibi_slopes()[gi]))
    o, lse = pl.pallas_call(
        kern,
        grid=(b, dil, nb),
        in_specs=[
            pl.BlockSpec((1, B_BLOCK, B_Q_DIM), lambda bi, r, j: (bi, j, r * N_GROUPS + gi)),
            pl.BlockSpec((1, B_BLOCK, B_KV_DIM), prev),
            pl.BlockSpec((1, B_BLOCK, B_KV_DIM), cur),
            pl.BlockSpec((1, B_BLOCK, B_KV_DIM), prev),
            pl.BlockSpec((1, B_BLOCK, B_KV_DIM), cur),
        ],
        out_specs=[
            pl.BlockSpec((1, B_BLOCK, B_Q_DIM), cur),
            pl.BlockSpec((1, 1, B_BLOCK, B_HEADS), lambda bi, r, j: (bi, r, j, 0)),
        ],
        out_shape=[
            jax.ShapeDtypeStruct((b, n_dec, dil * B_Q_DIM), BF16),
            jax.ShapeDtypeStruct((b, dil, n_dec, B_HEADS), F32),
        ],
        compiler_params=_params("parallel", "parallel", "arbitrary"),
        name=f"band_attn_g{gi}",
    )(qv, kv, kv, vv, vv)
    lse = jnp.swapaxes(lse, 1, 2).reshape(b * s, B_HEADS)
    return o.reshape(b * s, B_Q_DIM), lse


def _merge_kernel(o0_ref, o1_ref, o2_ref, lse_ref, ex_ref, y_ref):
    lse = lse_ref[...]
    ls = [lse[:, i * B_HEADS:(i + 1) * B_HEADS] for i in range(N_GROUPS)]
    mx = jnp.maximum(jnp.maximum(ls[0], ls[1]), ls[2])
    es = [jnp.exp(l - mx) for l in ls]
    inv = 1.0 / (es[0] + es[1] + es[2])
    ex = ex_ref[...]
    acc = None
    for e, o_ref in zip(es, (o0_ref, o1_ref, o2_ref)):
        w = e * inv
        w_hi = w.astype(BF16)
        w_lo = (w - w_hi.astype(F32)).astype(BF16)
        wide = (jnp.dot(w_hi, ex, preferred_element_type=F32)
                + jnp.dot(w_lo, ex, preferred_element_type=F32))
        term = wide * o_ref[...].astype(F32)
        acc = term if acc is None else acc + term
    y_ref[...] = acc.astype(y_ref.dtype)


def _merge(outs, lse, tm):
    m = lse.shape[0]
    tm = min(tm, m)
    ex = jnp.asarray(np.repeat(np.eye(B_HEADS, dtype=np.float32), B_DH, axis=1), BF16)
    row = lambda n: pl.BlockSpec((tm, n), lambda i: (i, 0))
    return pl.pallas_call(
        _merge_kernel,
        grid=(m // tm,),
        in_specs=[row(B_Q_DIM)] * 3 + [row(N_GROUPS * B_HEADS), _resident(ex.shape)],
        out_specs=row(B_Q_DIM),
        out_shape=jax.ShapeDtypeStruct((m, B_Q_DIM), BF16),
        compiler_params=_params("parallel"),
        name="merge_groups",
    )(*outs, lse, ex)


def _decode_bias(l_new, t_past):
    slopes = _alibi_slopes()
    rows = N_GROUPS * B_KV_HEADS * B_QPK * l_new
    bias_c = np.full((rows, t_past), NEG, np.float32)
    bias_n = np.full((rows, l_new), NEG, np.float32)
    pos_c = np.arange(t_past)
    pos_n = t_past + np.arange(l_new)
    r = 0
    for gi, (win, dil) in enumerate(B_GROUPS):
        for g in range(B_KV_HEADS):
            for p in range(B_QPK):
                for l in range(l_new):
                    for pos, tbl in ((pos_c, bias_c), (pos_n, bias_n)):
                        dist = t_past + l - pos
                        ok = (dist >= 0) & (dist <= win) & (dist % dil == 0)
                        tbl[r] = np.where(ok, -slopes[gi, g * B_QPK + p] * dist.astype(np.float32), NEG)
                    r += 1
    return bias_c, bias_n


def _decode_attn_kernel(q_ref, ck_ref, cv_ref, kn_ref, vn_ref, bc_ref, bn_ref, o_ref, *, l_new):
    rows_g = B_QPK * l_new
    rows_grp = B_KV_HEADS * rows_g
    qt = q_ref[0]
    lane = lax.broadcasted_iota(jnp.int32, (rows_g, B_KV_DIM), 1)
    blocks = []
    for gi in range(N_GROUPS):
        qg = qt[gi * rows_g:(gi + 1) * rows_g]
        for g in range(B_KV_HEADS):
            keep = (lane >= g * B_DH) & (lane < (g + 1) * B_DH)
            blocks.append(jnp.where(keep, qg, 0.0))
    qbd = jnp.concatenate(blocks, axis=0).astype(BF16)
    nt = (((1,), (1,)), ((), ()))
    s_c = lax.dot_general(qbd, ck_ref[0].astype(BF16), nt, preferred_element_type=F32) + bc_ref[...]
    s_n = lax.dot_general(qbd, kn_ref[0].astype(BF16), nt, preferred_element_type=F32) + bn_ref[...]
    mx = jnp.maximum(jnp.max(s_c, axis=-1, keepdims=True), jnp.max(s_n, axis=-1, keepdims=True))
    e_c = jnp.exp(s_c - mx)
    e_n = jnp.exp(s_n - mx)
    den = jnp.sum(e_c, axis=-1, keepdims=True) + jnp.sum(e_n, axis=-1, keepdims=True)
    acc = (jnp.dot(e_c.astype(BF16), cv_ref[0].astype(BF16), preferred_element_type=F32)
           + jnp.dot(e_n, vn_ref[0], preferred_element_type=F32))
    ms = [mx[i * rows_grp:(i + 1) * rows_grp] for i in range(N_GROUPS)]
    top = jnp.maximum(jnp.maximum(ms[0], ms[1]), ms[2])
    num, tot = None, None
    for i in range(N_GROUPS):
        a = jnp.exp(ms[i] - top)
        n_i = a * acc[i * rows_grp:(i + 1) * rows_grp]
        d_i = a * den[i * rows_grp:(i + 1) * rows_grp]
        num = n_i if num is None else num + n_i
        tot = d_i if tot is None else tot + d_i
    out = num / tot
    o_ref[0] = jnp.concatenate([out[g * rows_g:(g + 1) * rows_g, g * B_DH:(g + 1) * B_DH]
                                for g in range(B_KV_HEADS)], axis=0)


def _decode_attn(q, k_new, v_new, cache_k, cache_v, b, l_new):
    t_past = cache_k.shape[1]
    rows_g = B_QPK * l_new
    qt = q.reshape(b, l_new, N_GROUPS, B_KV_HEADS, B_QPK, B_DH)
    qt = jnp.transpose(qt, (0, 2, 4, 1, 3, 5)).reshape(b, N_GROUPS * rows_g, B_KV_DIM)
    bias_c, bias_n = (jnp.asarray(a) for a in _decode_bias(l_new, t_past))
    rows = bias_c.shape[0]
    seq = lambda n, d: pl.BlockSpec((1, n, d), lambda i: (i, 0, 0))
    out = pl.pallas_call(
        functools.partial(_decode_attn_kernel, l_new=l_new),
        grid=(b,),
        in_specs=[seq(N_GROUPS * rows_g, B_KV_DIM), seq(t_past, B_KV_DIM), seq(t_past, B_KV_DIM),
                  seq(l_new, B_KV_DIM), seq(l_new, B_KV_DIM), _resident(bias_c.shape), _resident(bias_n.shape)],
        out_specs=seq(B_KV_HEADS * rows_g, B_DH),
        out_shape=jax.ShapeDtypeStruct((b, B_KV_HEADS * rows_g, B_DH), F32),
        compiler_params=_params("parallel"),
        name="decode_attn",
    )(qt, cache_k.reshape(b, t_past, B_KV_DIM), cache_v.reshape(b, t_past, B_KV_DIM),
      k_new.reshape(b, l_new, B_KV_DIM), v_new.reshape(b, l_new, B_KV_DIM), bias_c, bias_n)
    out = out.reshape(b, B_HEADS, l_new, B_DH)
    return jnp.transpose(out, (0, 2, 1, 3)).reshape(b * l_new, B_Q_DIM).astype(BF16)


def _trunk(x3, conv_hist, delta_s, kv_past, w, *, tm_proj, tm_ffn, gdn_bt, gdn_chunk, gdn_mm):
    b, t, d = x3.shape
    m = b * t
    x = x3.reshape(m, d)
    new_hist, new_delta = [], []
    k_new = v_new = kb = vb = None
    for layer in range(DEPTH):
        if layer < N_A_LAYERS:
            pm, bd = _norm_proj(x, w["norms"][layer, 0:1], [w["a_w_main"][layer], w["a_w_gate"][layer]],
                                (0, 0), ((0, F32), (1, F32)), tm_proj)
            y, s_new, h_new = _gdn(pm.reshape(b, t, A_MAIN_DIM), bd.reshape(b, t, GATE_PAD),
                                   conv_hist[layer], delta_s[layer], w["a_conv_w"][layer], w["a_ab"][layer],
                                   w["a_o_gain"][layer], gdn_bt, gdn_chunk, gdn_mm)
            y = y.reshape(m, A_V_DIM)
            new_hist.append(h_new)
            new_delta.append(s_new)
            w_mix = w["a_w_out"][layer]
        else:
            j = layer - N_A_LAYERS
            q_dtype = BF16 if kv_past is None else F32
            if j == 0:
                gains = jnp.stack([w["kv_norm"], w["norms"][layer, 0]])
                k_new, v_new, kb, vb, q = _norm_proj(
                    x, gains, [w["b_w_k"], w["b_w_v"], w["b_w_q"][j]], (0, 0, 1),
                    ((0, F32), (1, F32), (0, BF16), (1, BF16), (2, q_dtype)), tm_proj)
            else:
                (q,) = _norm_proj(x, w["norms"][layer, 0:1], [w["b_w_q"][j]], (0,), ((0, q_dtype),), tm_proj)
            if kv_past is None:
                parts = [_band_attn(q, kb, vb, gi, b, t) for gi in range(N_GROUPS)]
                lse = jnp.concatenate([p[1] for p in parts], axis=1)
                y = _merge([p[0] for p in parts], lse, tm_ffn)
            else:
                y = _decode_attn(q, k_new, v_new, kv_past[0], kv_past[1], b, t)
            w_mix = w["b_w_o"][j]
        x = _post_ffn(x, y, w_mix, w["norms"][layer, 1:4], w["ffn_w_in"][layer], w["ffn_w_out"][layer], tm_ffn)
    return (x.reshape(b, t, d), jnp.stack(new_hist), jnp.stack(new_delta),
            k_new.reshape(b, t, B_KV_HEADS, B_DH), v_new.reshape(b, t, B_KV_HEADS, B_DH))


def kernel(x_prompt, x_sample, state_conv, state_delta, cache_k, cache_v, norms, kv_norm, a_w_in,
           a_conv_w, a_log, a_dt_bias, a_o_gain, a_w_out, b_w_kv, b_w_q, b_w_o, ffn_w_in, ffn_w_out):
    bp, sp, _ = x_prompt.shape
    gate_w = a_w_in[:, :, A_MAIN_DIM:]
    gate_w = jnp.pad(gate_w, ((0, 0), (0, 0), (0, GATE_PAD - gate_w.shape[-1])))
    ab = jnp.pad(jnp.stack([a_log, a_dt_bias], axis=1), ((0, 0), (0, 0), (A_HEADS, GATE_PAD - 2 * A_HEADS)))
    w_kv = b_w_kv.reshape(D_MODEL, 2, B_KV_DIM)
    w = {
        "norms": norms, "kv_norm": kv_norm,
        "a_w_main": a_w_in[:, :, :A_MAIN_DIM].astype(BF16), "a_w_gate": gate_w.astype(BF16),
        "a_conv_w": a_conv_w, "a_ab": ab, "a_o_gain": a_o_gain.reshape(N_A_LAYERS, 1, A_DV),
        "a_w_out": a_w_out.astype(BF16),
        "b_w_k": w_kv[:, 0].astype(BF16), "b_w_v": w_kv[:, 1].astype(BF16),
        "b_w_q": (b_w_q * (B_DH ** -0.5)).astype(BF16), "b_w_o": b_w_o.astype(BF16),
        "ffn_w_in": ffn_w_in.astype(BF16), "ffn_w_out": ffn_w_out.astype(BF16),
    }
    zero_hist = jnp.zeros((N_A_LAYERS, bp, CONV_W - 1, A_CONV_DIM), x_prompt.dtype)
    zero_delta = jnp.zeros((N_A_LAYERS, bp, A_HEADS, A_DK, A_DV), x_prompt.dtype)
    y_p, conv_p, delta_p, k_p, v_p = _trunk(
        x_prompt, zero_hist, zero_delta, None, w,
        tm_proj=256, tm_ffn=512, gdn_bt=1, gdn_chunk=64, gdn_mm=BF16)
    y_s, conv_s, delta_s, k_s, v_s = _trunk(
        x_sample, state_conv, state_delta, (cache_k, cache_v), w,
        tm_proj=256, tm_ffn=512, gdn_bt=4, gdn_chunk=x_sample.shape[1], gdn_mm=F32)
    win_p = min(MAX_WINDOW, sp)
    return (y_p, y_s, conv_p, delta_p, k_p[:, sp - win_p:], v_p[:, sp - win_p:], conv_s, delta_s, k_s, v_s)
```

```python
import functools
import math

import jax
import jax.numpy as jnp
import numpy as np
from jax import lax
from jax.experimental import pallas as pl
from jax.experimental.pallas import tpu as pltpu

F32 = jnp.float32
BF16 = jnp.bfloat16

D_MODEL = 1024
DEPTH = 4
N_A_LAYERS = 2
EPS = 1e-6

A_HEADS = 8
A_DK = 128
A_DV = 128
CONV_W = 4
A_QK_DIM = A_HEADS * A_DK
A_V_DIM = A_HEADS * A_DV
A_CONV_DIM = 2 * A_QK_DIM + A_V_DIM
A_MAIN_DIM = A_CONV_DIM + A_V_DIM
GATE_PAD = 128

B_GROUPS = ((128, 1), (512, 4), (2048, 16))
N_GROUPS = 3
B_HEADS = 16
B_DH = 64
B_KV_HEADS = 4
B_QPK = B_HEADS // B_KV_HEADS
B_BLOCK = 128
B_KV_DIM = B_KV_HEADS * B_DH
B_Q_DIM = B_HEADS * B_DH
MAX_WINDOW = 2048

D_FF = 2816
FF_CHUNK = 256

LANES = 128
NEG = -1e30
V7X_VMEM_LIMIT = 56 * 1024 * 1024
HIGHEST = lax.Precision.HIGHEST


def _params(*sem):
    return pltpu.CompilerParams(dimension_semantics=sem, vmem_limit_bytes=V7X_VMEM_LIMIT)


def _resident(shape):
    nd = len(shape)
    return pl.BlockSpec(shape, lambda *_: (0,) * nd, pipeline_mode=pl.Buffered(1))


def _rms(x):
    return x * lax.rsqrt(jnp.mean(x * x, axis=-1, keepdims=True) + EPS)


def _silu(x):
    return x * jax.nn.sigmoid(x)


def _norm_proj_kernel(x_ref, g_ref, *refs, group_of, outs, n_chunk):
    nw = len(group_of)
    w_refs, o_refs = refs[:nw], refs[nw:]
    xs = _rms(x_ref[...])
    normed = {}
    for wi, grp in enumerate(group_of):
        if grp not in normed:
            normed[grp] = (xs * g_ref[grp:grp + 1, :]).astype(BF16)
        xn = normed[grp]
        n = w_refs[wi].shape[1]
        for n0 in range(0, n, n_chunk):
            n1 = min(n0 + n_chunk, n)
            res = jnp.dot(xn, w_refs[wi][:, n0:n1], preferred_element_type=F32)
            for oi, (src, _) in enumerate(outs):
                if src == wi:
                    o_refs[oi][:, n0:n1] = res.astype(o_refs[oi].dtype)


def _norm_proj(x, gains, weights, group_of, outs, tm):
    m, d = x.shape
    tm = min(tm, m)
    kern = functools.partial(_norm_proj_kernel, group_of=tuple(group_of), outs=tuple(outs), n_chunk=512)
    return pl.pallas_call(
        kern,
        grid=(m // tm,),
        in_specs=[pl.BlockSpec((tm, d), lambda i: (i, 0)), _resident(gains.shape)]
        + [_resident(w.shape) for w in weights],
        out_specs=[pl.BlockSpec((tm, weights[src].shape[1]), lambda i: (i, 0)) for src, _ in outs],
        out_shape=[jax.ShapeDtypeStruct((m, weights[src].shape[1]), dt) for src, dt in outs],
        compiler_params=_params("parallel"),
        name="norm_proj",
    )(x, gains, *weights)


def _gdn_kernel(pm_ref, bd_ref, hist_ref, s0_ref, cw_ref, ab_ref, og_ref,
                o_ref, s_ref, hnew_ref, xh_ref, *, bt, chunk, mm_dtype):
    nh, c = A_HEADS, chunk
    bh = bt * nh

    @pl.when(pl.program_id(1) == 0)
    def _start():
        s_ref[...] = s0_ref[...]
        xh_ref[:, 5:8, :] = hist_ref[...]

    x = pm_ref[:, :, 0:A_CONV_DIM]
    xh_ref[:, 8:8 + c, :] = x
    cw = cw_ref[...]
    conv = x * cw[CONV_W - 1:CONV_W, :]
    for j in range(1, CONV_W):
        conv = conv + xh_ref[:, 8 - j:8 - j + c, :] * cw[CONV_W - 1 - j:CONV_W - j, :]
    tail = xh_ref[:, 5 + c:8 + c, :]
    xh_ref[:, 5:8, :] = tail
    hnew_ref[...] = tail
    act = _silu(conv)

    def heads(base):
        return jnp.stack([act[bi, :, base + h * A_DK: base + (h + 1) * A_DK]
                          for bi in range(bt) for h in range(nh)])

    qr, kr, v = heads(0), heads(A_QK_DIM), heads(2 * A_QK_DIM)
    q = qr * (lax.rsqrt(jnp.sum(qr * qr, axis=-1, keepdims=True) + EPS) * (A_DK ** -0.5))
    k = kr * lax.rsqrt(jnp.sum(kr * kr, axis=-1, keepdims=True) + EPS)

    bd = bd_ref[...]
    beta_all = jax.nn.sigmoid(bd)
    z = bd + ab_ref[1:2, :]
    softplus = jnp.maximum(z, 0.0) + jnp.log1p(jnp.exp(-jnp.abs(z)))
    g_all = -jnp.exp(ab_ref[0:1, :]) * softplus

    ti = lax.broadcasted_iota(jnp.int32, (c, c), 0)
    si = lax.broadcasted_iota(jnp.int32, (c, c), 1)
    incl = ti >= si
    strict = ti > si
    tri = incl.astype(F32)
    gcol, grow, bcol, glast = [], [], [], []
    for bi in range(bt):
        gc = jnp.dot(tri, g_all[bi], precision=HIGHEST, preferred_element_type=F32)
        gct = gc.T
        for h in range(nh):
            gcol.append(gc[:, nh + h:nh + h + 1])
            grow.append(gct[nh + h:nh + h + 1, :])
            glast.append(gc[c - 1:c, nh + h:nh + h + 1])
            bcol.append(beta_all[bi][:, h:h + 1])
    gcol, grow, bcol, glast = jnp.stack(gcol), jnp.stack(grow), jnp.stack(bcol), jnp.stack(glast)

    decay = jnp.exp(jnp.where(incl, gcol - grow, NEG))
    gam = jnp.exp(gcol)

    def bmm(a, b):
        return jnp.einsum("bij,bjk->bik", a.astype(mm_dtype), b.astype(mm_dtype),
                          preferred_element_type=F32)

    def bmm_nt(a, b):
        return jnp.einsum("bid,bjd->bij", a.astype(mm_dtype), b.astype(mm_dtype),
                          preferred_element_type=F32)

    kk = bmm_nt(k, k)
    qk = bmm_nt(q, k)
    m_pow = jnp.where(strict, -(bcol * decay * kk), 0.0)
    q_inv = m_pow
    for _ in range(int(math.log2(c)) - 1):
        m_pow = bmm(m_pow, m_pow)
        q_inv = q_inv + m_pow + bmm(q_inv, m_pow)
    rhs = jnp.concatenate([(bcol * gam) * k, bcol * v], axis=-1)
    sol = rhs + bmm(q_inv, rhs)
    w_mat, u_base = sol[..., :A_DK], sol[..., A_DK:]

    s_old = s_ref[...].reshape(bh, A_DK, A_DV)
    u = u_base - bmm(w_mat, s_old)
    o = bmm(gam * q, s_old) + bmm(decay * qk, u)
    k_dec = jnp.exp(glast - gcol) * k
    s_new = jnp.exp(glast) * s_old + bmm(jnp.swapaxes(k_dec, 1, 2), u)
    s_ref[...] = s_new.reshape(bt, nh, A_DK, A_DV)

    on = _rms(o) * og_ref[...]
    for bi in range(bt):
        for h in range(nh):
            gate = pm_ref[bi, :, A_CONV_DIM + h * A_DV:A_CONV_DIM + (h + 1) * A_DV]
            o_ref[bi, :, h * A_DV:(h + 1) * A_DV] = (on[bi * nh + h] * _silu(gate)).astype(o_ref.dtype)


def _gdn(pm, bd, hist, s0, layer, conv_w, ab, o_gain, bt, chunk, mm_dtype):
    b, t, _ = pm.shape
    kern = functools.partial(_gdn_kernel, bt=bt, chunk=chunk, mm_dtype=mm_dtype)
    return pl.pallas_call(
        kern,
        grid=(b // bt, t // chunk),
        in_specs=[
            pl.BlockSpec((bt, chunk, A_MAIN_DIM), lambda i, j: (i, j, 0)),
            pl.BlockSpec((bt, chunk, GATE_PAD), lambda i, j: (i, j, 0)),
            pl.BlockSpec((None, bt, CONV_W - 1, A_CONV_DIM), lambda i, j: (layer, i, 0, 0)),
            pl.BlockSpec((None, bt, A_HEADS, A_DK, A_DV), lambda i, j: (layer, i, 0, 0, 0)),
            pl.BlockSpec((CONV_W, A_CONV_DIM), lambda i, j: (0, 0)),
            pl.BlockSpec((2, GATE_PAD), lambda i, j: (0, 0)),
            pl.BlockSpec((1, A_DV), lambda i, j: (0, 0)),
        ],
        out_specs=[
            pl.BlockSpec((bt, chunk, A_V_DIM), lambda i, j: (i, j, 0)),
            pl.BlockSpec((bt, A_HEADS, A_DK, A_DV), lambda i, j: (i, 0, 0, 0)),
            pl.BlockSpec((bt, CONV_W - 1, A_CONV_DIM), lambda i, j: (i, 0, 0)),
        ],
        out_shape=[
            jax.ShapeDtypeStruct((b, t, A_V_DIM), BF16),
            jax.ShapeDtypeStruct((b, A_HEADS, A_DK, A_DV), F32),
            jax.ShapeDtypeStruct((b, CONV_W - 1, A_CONV_DIM), F32),
        ],
        scratch_shapes=[pltpu.VMEM((bt, chunk + 8, A_CONV_DIM), F32)],
        compiler_params=_params("parallel", "arbitrary"),
        name="gdn_mixer",
    )(pm, bd, hist, s0, conv_w, ab, o_gain)


def _post_ffn_kernel(x_ref, y_ref, wo_ref, g_ref, wi_ref, w2_ref, out_ref, acc_ref):
    y = jnp.dot(y_ref[...], wo_ref[...], preferred_element_type=F32)
    x1 = x_ref[...] + _rms(y) * g_ref[0:1, :]
    out_ref[...] = x1
    hn = (_rms(x1) * g_ref[1:2, :]).astype(BF16)
    for j in range(0, D_FF, FF_CHUNK):
        gate = jnp.dot(hn, wi_ref[:, j:j + FF_CHUNK], preferred_element_type=F32)
        up = jnp.dot(hn, wi_ref[:, D_FF + j:D_FF + j + FF_CHUNK], preferred_element_type=F32)
        part = jnp.dot((_silu(gate) * up).astype(BF16), w2_ref[j:j + FF_CHUNK, :],
                       preferred_element_type=F32)
        if j == 0:
            acc_ref[...] = part
        else:
            acc_ref[...] += part
    out_ref[...] += _rms(acc_ref[...]) * g_ref[2:3, :]


def _post_ffn(x, y, w_o, gains, w_in, w_out, tm):
    m, d = x.shape
    tm = min(tm, m)
    return pl.pallas_call(
        _post_ffn_kernel,
        grid=(m // tm,),
        in_specs=[
            pl.BlockSpec((tm, d), lambda i: (i, 0)),
            pl.BlockSpec((tm, y.shape[1]), lambda i: (i, 0)),
            _resident(w_o.shape), _resident(gains.shape), _resident(w_in.shape), _resident(w_out.shape),
        ],
        out_specs=pl.BlockSpec((tm, d), lambda i: (i, 0)),
        out_shape=jax.ShapeDtypeStruct((m, d), F32),
        scratch_shapes=[pltpu.VMEM((tm, d), F32)],
        compiler_params=_params("parallel"),
        name="post_ffn",
    )(x, y, w_o, gains, w_in, w_out)


def _decimate_rows(src_ref, n_chunks, dil, rows):
    return jnp.concatenate(
        [jnp.concatenate([src_ref[c, pl.ds(r, rows, stride=dil), :] for c in range(n_chunks)], axis=1)
         for r in range(dil)], axis=0)


def _b_proj_kernel(x_ref, g_ref, *refs, with_kv, ts):
    if with_kv:
        wk_ref, wv_ref, wq_ref, k_ref, v_ref = refs[:5]
        kd_refs, vd_refs, qd_refs = refs[5:8], refs[8:11], refs[11:14]
        xn_scr, kv_scr = refs[14:]
    else:
        wq_ref = refs[0]
        qd_refs = refs[1:4]
        (xn_scr,) = refs[4:]
    xs = _rms(x_ref[...])
    q_gain = 1 if with_kv else 0
    xq = xs * g_ref[q_gain:q_gain + 1, :]
    nq = D_MODEL // LANES
    for c in range(nq):
        xn_scr[c] = xq[:, c * LANES:(c + 1) * LANES]
    for gi, (_, dil) in enumerate(B_GROUPS):
        rows = ts // dil
        lhs = (xq if dil == 1 else _decimate_rows(xn_scr, nq, dil, rows)).astype(BF16)
        for n0 in range(0, B_Q_DIM, 512):
            res = jnp.dot(lhs, wq_ref[:, gi * B_Q_DIM + n0:gi * B_Q_DIM + n0 + 512], preferred_element_type=F32)
            qd_refs[gi][0, :, :, n0:n0 + 512] = res.reshape(dil, rows, 512).astype(BF16)
    if with_kv:
        xkv = (xs * g_ref[0:1, :]).astype(BF16)
        nk = B_KV_DIM // LANES
        for w_ref, nat_ref, d_refs in ((wk_ref, k_ref, kd_refs), (wv_ref, v_ref, vd_refs)):
            val = jnp.dot(xkv, w_ref[...], preferred_element_type=F32)
            nat_ref[...] = val
            for c in range(nk):
                kv_scr[c] = val[:, c * LANES:(c + 1) * LANES]
            for gi, (_, dil) in enumerate(B_GROUPS):
                rows = ts // dil
                dec = val if dil == 1 else _decimate_rows(kv_scr, nk, dil, rows)
                d_refs[gi][0] = dec.reshape(dil, rows, B_KV_DIM).astype(BF16)


def _b_proj(x, gains, weights, b, s, with_kv, ts):
    m, d = x.shape
    tiles = s // ts
    dec = lambda n: [pl.BlockSpec((1, dil, ts // dil, n), lambda i: (i // tiles, 0, i % tiles, 0))
                     for _, dil in B_GROUPS]
    dec_shape = lambda n: [jax.ShapeDtypeStruct((b, dil, s // dil, n), BF16) for _, dil in B_GROUPS]
    nat = pl.BlockSpec((ts, B_KV_DIM), lambda i: (i, 0))
    out_specs, out_shape = dec(B_Q_DIM), dec_shape(B_Q_DIM)
    scratch = [pltpu.VMEM((D_MODEL // LANES, ts, LANES), F32)]
    if with_kv:
        out_specs = [nat, nat] + dec(B_KV_DIM) + dec(B_KV_DIM) + out_specs
        out_shape = [jax.ShapeDtypeStruct((m, B_KV_DIM), F32)] * 2 + dec_shape(B_KV_DIM) * 2 + out_shape
        scratch.append(pltpu.VMEM((B_KV_DIM // LANES, ts, LANES), F32))
    return pl.pallas_call(
        functools.partial(_b_proj_kernel, with_kv=with_kv, ts=ts),
        grid=(m // ts,),
        in_specs=[pl.BlockSpec((ts, d), lambda i: (i, 0)), _resident(gains.shape)]
        + [_resident(w.shape) for w in weights],
        out_specs=out_specs,
        out_shape=out_shape,
        scratch_shapes=scratch,
        compiler_params=_params("parallel"),
        name="attn_proj",
    )(x, gains, *weights)


def _alibi_slopes():
    n = N_GROUPS * B_HEADS
    return (2.0 ** (-8.0 * np.arange(1, n + 1) / n)).astype(np.float32).reshape(N_GROUPS, B_HEADS)


def _band_bias(gi):
    _, dil = B_GROUPS[gi]
    qi = np.arange(B_BLOCK)[:, None]
    ci = np.arange(B_BLOCK)[None, :]
    delta = np.where(ci > qi, B_BLOCK + qi - ci, qi - ci).astype(np.float32)
    rest = -_alibi_slopes()[gi][:, None, None] * (dil * delta)[None]
    first = np.where((ci > qi)[None], np.float32(NEG), rest)
    return np.stack([first, rest]).astype(np.float32)


def _band_attn_kernel(q_ref, kp_ref, kc_ref, vp_ref, vc_ref, bias_ref, o_ref, st_ref, *, far_bias):
    blk = pl.program_id(2)
    first = blk == 0
    sel = jnp.minimum(blk, 1)
    qi = lax.broadcasted_iota(jnp.int32, (B_BLOCK, B_BLOCK), 0)
    ci = lax.broadcasted_iota(jnp.int32, (B_BLOCK, B_BLOCK), 1)
    upper = ci > qi
    diag = ci == qi
    q = q_ref[0, 0]
    kp, kc, vp, vc = kp_ref[0, 0], kc_ref[0, 0], vp_ref[0, 0], vc_ref[0, 0]
    nt = (((1,), (1,)), ((), ()))
    for g in range(B_KV_HEADS):
        cols = slice(g * B_DH, (g + 1) * B_DH)
        qg = jnp.concatenate([q[:, (g * B_QPK + p) * B_DH:(g * B_QPK + p + 1) * B_DH]
                              for p in range(B_QPK)], axis=0)
        s_prev = lax.dot_general(qg, kp[:, cols], nt, preferred_element_type=F32)
        s_cur = lax.dot_general(qg, kc[:, cols], nt, preferred_element_type=F32)
        e_up, e_lo = [], []
        for p in range(B_QPK):
            h = g * B_QPK + p
            rows = slice(p * B_BLOCK, (p + 1) * B_BLOCK)
            sp = s_prev[rows]
            s = jnp.where(upper, sp, s_cur[rows]) + bias_ref[sel, h]
            far = (jnp.sum(jnp.where(diag, sp, 0.0), axis=-1, keepdims=True)
                   + jnp.where(first, NEG, far_bias[h]))
            mx = jnp.maximum(jnp.max(s, axis=-1, keepdims=True), far)
            e = jnp.exp(s - mx)
            e_far = jnp.exp(far - mx)
            st_ref[0, 0, :, h:h + 1] = mx
            st_ref[0, 0, :, B_HEADS + h:B_HEADS + h + 1] = jnp.sum(e, axis=-1, keepdims=True) + e_far
            e_up.append(jnp.where(upper, e, jnp.where(diag, e_far, 0.0)).astype(BF16))
            e_lo.append(jnp.where(upper, 0.0, e).astype(BF16))
        pv = (jnp.dot(jnp.concatenate(e_up, axis=0), vp[:, cols], preferred_element_type=F32)
              + jnp.dot(jnp.concatenate(e_lo, axis=0), vc[:, cols], preferred_element_type=F32))
        for p in range(B_QPK):
            h = g * B_QPK + p
            o_ref[0, 0, :, h * B_DH:(h + 1) * B_DH] = pv[p * B_BLOCK:(p + 1) * B_BLOCK].astype(o_ref.dtype)


def _band_attn(qd, kd, vd, gi):
    win, dil = B_GROUPS[gi]
    assert win // dil == B_BLOCK
    b, _, n_dec, _ = qd.shape
    bias = jnp.asarray(_band_bias(gi))
    far_bias = tuple(float(-sl * win) for sl in _alibi_slopes()[gi])
    cur = lambda bi, r, j: (bi, r, j, 0)
    prev = lambda bi, r, j: (bi, r, jnp.maximum(j - 1, 0), 0)
    blk = lambda n, imap: pl.BlockSpec((1, 1, B_BLOCK, n), imap)
    return pl.pallas_call(
        functools.partial(_band_attn_kernel, far_bias=far_bias),
        grid=(b, dil, n_dec // B_BLOCK),
        in_specs=[blk(B_Q_DIM, cur), blk(B_KV_DIM, prev), blk(B_KV_DIM, cur), blk(B_KV_DIM, prev),
                  blk(B_KV_DIM, cur), _resident(bias.shape)],
        out_specs=[blk(B_Q_DIM, cur), blk(2 * B_HEADS, cur)],
        out_shape=[jax.ShapeDtypeStruct((b, dil, n_dec, B_Q_DIM), BF16),
                   jax.ShapeDtypeStruct((b, dil, n_dec, 2 * B_HEADS), F32)],
        compiler_params=_params("parallel", "parallel", "arbitrary"),
        name=f"band_attn_g{gi}",
    )(qd, kd, kd, vd, vd, bias)


def _merge_kernel(a0_ref, a1_ref, a2_ref, st_ref, ex_ref, y_ref, nat_scr, *, tm):
    st = st_ref[...]
    ms = [st[:, g * 2 * B_HEADS:g * 2 * B_HEADS + B_HEADS] for g in range(N_GROUPS)]
    ls = [st[:, g * 2 * B_HEADS + B_HEADS:(g + 1) * 2 * B_HEADS] for g in range(N_GROUPS)]
    top = jnp.maximum(jnp.maximum(ms[0], ms[1]), ms[2])
    es = [jnp.exp(m - top) for m in ms]
    inv = 1.0 / (es[0] * ls[0] + es[1] * ls[1] + es[2] * ls[2])
    ex = ex_ref[...]
    nq = B_Q_DIM // LANES
    acc = None
    for (_, dil), e, a_ref in zip(B_GROUPS, es, (a0_ref, a1_ref, a2_ref)):
        w = e * inv
        w_hi = w.astype(BF16)
        w_lo = (w - w_hi.astype(F32)).astype(BF16)
        wide = (jnp.dot(w_hi, ex, preferred_element_type=F32)
                + jnp.dot(w_lo, ex, preferred_element_type=F32))
        if dil == 1:
            part = a_ref[0, 0].astype(F32)
        else:
            rows = tm // dil
            for r in range(dil):
                val = a_ref[0, r].astype(F32)
                for c in range(nq):
                    nat_scr[c, pl.ds(r, rows, stride=dil), :] = val[:, c * LANES:(c + 1) * LANES]
            part = jnp.concatenate([nat_scr[c] for c in range(nq)], axis=1)
        term = wide * part
        acc = term if acc is None else acc + term
    y_ref[...] = acc.astype(y_ref.dtype)


def _merge(accs, stats, s, tm):
    m = stats.shape[0]
    tiles = s // tm
    ex = jnp.asarray(np.repeat(np.eye(B_HEADS, dtype=np.float32), B_DH, axis=1), BF16)
    return pl.pallas_call(
        functools.partial(_merge_kernel, tm=tm),
        grid=(m // tm,),
        in_specs=[pl.BlockSpec((1, dil, tm // dil, B_Q_DIM), lambda i: (i // tiles, 0, i % tiles, 0))
                  for _, dil in B_GROUPS]
        + [pl.BlockSpec((tm, stats.shape[1]), lambda i: (i, 0)), _resident(ex.shape)],
        out_specs=pl.BlockSpec((tm, B_Q_DIM), lambda i: (i, 0)),
        out_shape=jax.ShapeDtypeStruct((m, B_Q_DIM), BF16),
        scratch_shapes=[pltpu.VMEM((B_Q_DIM // LANES, tm, LANES), F32)],
        compiler_params=_params("parallel"),
        name="merge_groups",
    )(*accs, stats, ex)


def _decode_bias(l_new, t_past):
    slopes = _alibi_slopes()
    rows = N_GROUPS * B_KV_HEADS * B_QPK * l_new
    bias_c = np.full((rows, t_past), NEG, np.float32)
    bias_n = np.full((rows, l_new), NEG, np.float32)
    pos_c = np.arange(t_past)
    pos_n = t_past + np.arange(l_new)
    r = 0
    for gi, (win, dil) in enumerate(B_GROUPS):
        for g in range(B_KV_HEADS):
            for p in range(B_QPK):
                for l in range(l_new):
                    for pos, tbl in ((pos_c, bias_c), (pos_n, bias_n)):
                        dist = t_past + l - pos
                        ok = (dist >= 0) & (dist <= win) & (dist % dil == 0)
                        tbl[r] = np.where(ok, -slopes[gi, g * B_QPK + p] * dist.astype(np.float32), NEG)
                    r += 1
    return bias_c, bias_n


def _decode_attn_kernel(q_ref, ck_ref, cv_ref, kn_ref, vn_ref, bc_ref, bn_ref, o_ref, *, l_new):
    rows_g = B_QPK * l_new
    rows_grp = B_KV_HEADS * rows_g
    qt = q_ref[0]
    lane = lax.broadcasted_iota(jnp.int32, (rows_g, B_KV_DIM), 1)
    blocks = []
    for gi in range(N_GROUPS):
        qg = qt[gi * rows_g:(gi + 1) * rows_g]
        for g in range(B_KV_HEADS):
            keep = (lane >= g * B_DH) & (lane < (g + 1) * B_DH)
            blocks.append(jnp.where(keep, qg, 0.0))
    qbd = jnp.concatenate(blocks, axis=0).astype(BF16)
    nt = (((1,), (1,)), ((), ()))
    s_c = lax.dot_general(qbd, ck_ref[0].astype(BF16), nt, preferred_element_type=F32) + bc_ref[...]
    s_n = lax.dot_general(qbd, kn_ref[0].astype(BF16), nt, preferred_element_type=F32) + bn_ref[...]
    mx = jnp.maximum(jnp.max(s_c, axis=-1, keepdims=True), jnp.max(s_n, axis=-1, keepdims=True))
    e_c = jnp.exp(s_c - mx)
    e_n = jnp.exp(s_n - mx)
    den = jnp.sum(e_c, axis=-1, keepdims=True) + jnp.sum(e_n, axis=-1, keepdims=True)
    acc = (jnp.dot(e_c.astype(BF16), cv_ref[0].astype(BF16), preferred_element_type=F32)
           + jnp.dot(e_n, vn_ref[0], preferred_element_type=F32))
    ms = [mx[i * rows_grp:(i + 1) * rows_grp] for i in range(N_GROUPS)]
    top = jnp.maximum(jnp.maximum(ms[0], ms[1]), ms[2])
    num, tot = None, None
    for i in range(N_GROUPS):
        a = jnp.exp(ms[i] - top)
        n_i = a * acc[i * rows_grp:(i + 1) * rows_grp]
        d_i = a * den[i * rows_grp:(i + 1) * rows_grp]
        num = n_i if num is None else num + n_i
        tot = d_i if tot is None else tot + d_i
    out = num / tot
    o_ref[0] = jnp.concatenate([out[g * rows_g:(g + 1) * rows_g, g * B_DH:(g + 1) * B_DH]
                                for g in range(B_KV_HEADS)], axis=0)


def _decode_attn(q, k_new, v_new, cache_k, cache_v, b, l_new):
    t_past = cache_k.shape[1]
    rows_g = B_QPK * l_new
    qt = q.reshape(b, l_new, N_GROUPS, B_KV_HEADS, B_QPK, B_DH)
    qt = jnp.transpose(qt, (0, 2, 4, 1, 3, 5)).reshape(b, N_GROUPS * rows_g, B_KV_DIM)
    bias_c, bias_n = (jnp.asarray(a) for a in _decode_bias(l_new, t_past))
    seq = lambda n, d: pl.BlockSpec((1, n, d), lambda i: (i, 0, 0))
    out = pl.pallas_call(
        functools.partial(_decode_attn_kernel, l_new=l_new),
        grid=(b,),
        in_specs=[seq(N_GROUPS * rows_g, B_KV_DIM), seq(t_past, B_KV_DIM), seq(t_past, B_KV_DIM),
                  seq(l_new, B_KV_DIM), seq(l_new, B_KV_DIM), _resident(bias_c.shape), _resident(bias_n.shape)],
        out_specs=seq(B_KV_HEADS * rows_g, B_DH),
        out_shape=jax.ShapeDtypeStruct((b, B_KV_HEADS * rows_g, B_DH), F32),
        compiler_params=_params("parallel"),
        name="decode_attn",
    )(qt, cache_k.reshape(b, t_past, B_KV_DIM), cache_v.reshape(b, t_past, B_KV_DIM),
      k_new.reshape(b, l_new, B_KV_DIM), v_new.reshape(b, l_new, B_KV_DIM), bias_c, bias_n)
    out = out.reshape(b, B_HEADS, l_new, B_DH)
    return jnp.transpose(out, (0, 2, 1, 3)).reshape(b * l_new, B_Q_DIM).astype(BF16)


def _token_order(dec, s):
    b, _, _, n = dec.shape
    return jnp.swapaxes(dec, 1, 2).reshape(b * s, n)


def _trunk(x3, conv_hist, delta_s, kv_past, w, *, tm_proj, tm_ffn, gdn_bt, gdn_chunk, gdn_mm):
    b, t, d = x3.shape
    m = b * t
    x = x3.reshape(m, d)
    new_hist, new_delta = [], []
    k_new = v_new = kd = vd = None
    for layer in range(DEPTH):
        if layer < N_A_LAYERS:
            pm, bd = _norm_proj(x, w["norms"][layer, 0:1], [w["a_w_main"][layer], w["a_w_gate"][layer]],
                                (0, 0), ((0, F32), (1, F32)), tm_proj)
            y, s_new, h_new = _gdn(pm.reshape(b, t, A_MAIN_DIM), bd.reshape(b, t, GATE_PAD),
                                   conv_hist, delta_s, layer, w["a_conv_w"][layer], w["a_ab"][layer],
                                   w["a_o_gain"][layer], gdn_bt, gdn_chunk, gdn_mm)
            y = y.reshape(m, A_V_DIM)
            new_hist.append(h_new)
            new_delta.append(s_new)
            w_mix = w["a_w_out"][layer]
        else:
            j = layer - N_A_LAYERS
            q_gain = w["norms"][layer, 0]
            if kv_past is None:
                if j == 0:
                    res = _b_proj(x, jnp.stack([w["kv_norm"], q_gain]), [w["b_w_k"], w["b_w_v"], w["b_w_q"][j]],
                                  b, t, True, tm_ffn)
                    k_new, v_new, kd, vd, qd = res[0], res[1], res[2:5], res[5:8], res[8:11]
                else:
                    qd = _b_proj(x, q_gain[None], [w["b_w_q"][j]], b, t, False, tm_ffn)
                parts = [_band_attn(qd[gi], kd[gi], vd[gi], gi) for gi in range(N_GROUPS)]
                stats = jnp.concatenate([_token_order(p[1], t) for p in parts], axis=1)
                y = _merge([p[0] for p in parts], stats, t, tm_ffn)
            else:
                if j == 0:
                    k_new, v_new, q = _norm_proj(
                        x, jnp.stack([w["kv_norm"], q_gain]), [w["b_w_k"], w["b_w_v"], w["b_w_q"][j]],
                        (0, 0, 1), ((0, F32), (1, F32), (2, F32)), tm_proj)
                else:
                    (q,) = _norm_proj(x, q_gain[None], [w["b_w_q"][j]], (0,), ((0, F32),), tm_proj)
                y = _decode_attn(q, k_new, v_new, kv_past[0], kv_past[1], b, t)
            w_mix = w["b_w_o"][j]
        x = _post_ffn(x, y, w_mix, w["norms"][layer, 1:4], w["ffn_w_in"][layer], w["ffn_w_out"][layer], tm_ffn)
    return (x.reshape(b, t, d), jnp.stack(new_hist), jnp.stack(new_delta),
            k_new.reshape(b, t, B_KV_HEADS, B_DH), v_new.reshape(b, t, B_KV_HEADS, B_DH))


def kernel(x_prompt, x_sample, state_conv, state_delta, cache_k, cache_v, norms, kv_norm, a_w_in,
           a_conv_w, a_log, a_dt_bias, a_o_gain, a_w_out, b_w_kv, b_w_q, b_w_o, ffn_w_in, ffn_w_out):
    bp, sp, _ = x_prompt.shape
    gate_w = a_w_in[:, :, A_MAIN_DIM:]
    gate_w = jnp.pad(gate_w, ((0, 0), (0, 0), (0, GATE_PAD - gate_w.shape[-1])))
    ab = jnp.pad(jnp.stack([a_log, a_dt_bias], axis=1), ((0, 0), (0, 0), (A_HEADS, GATE_PAD - 2 * A_HEADS)))
    w_kv = b_w_kv.reshape(D_MODEL, 2, B_KV_DIM)
    w = {
        "norms": norms, "kv_norm": kv_norm,
        "a_w_main": a_w_in[:, :, :A_MAIN_DIM].astype(BF16), "a_w_gate": gate_w.astype(BF16),
        "a_conv_w": a_conv_w, "a_ab": ab, "a_o_gain": a_o_gain.reshape(N_A_LAYERS, 1, A_DV),
        "a_w_out": a_w_out.astype(BF16),
        "b_w_k": w_kv[:, 0].astype(BF16), "b_w_v": w_kv[:, 1].astype(BF16),
        "b_w_q": (b_w_q * (B_DH ** -0.5)).astype(BF16), "b_w_o": b_w_o.astype(BF16),
        "ffn_w_in": ffn_w_in.astype(BF16), "ffn_w_out": ffn_w_out.astype(BF16),
    }
    zero_hist = jnp.zeros((N_A_LAYERS, bp, CONV_W - 1, A_CONV_DIM), x_prompt.dtype)
    zero_delta = jnp.zeros((N_A_LAYERS, bp, A_HEADS, A_DK, A_DV), x_prompt.dtype)
    y_p, conv_p, delta_p, k_p, v_p = _trunk(
        x_prompt, zero_hist, zero_delta, None, w,
        tm_proj=256, tm_ffn=512, gdn_bt=1, gdn_chunk=64, gdn_mm=BF16)
    y_s, conv_s, delta_s, k_s, v_s = _trunk(
        x_sample, state_conv, state_delta, (cache_k, cache_v), w,
        tm_proj=256, tm_ffn=512, gdn_bt=4, gdn_chunk=x_sample.shape[1], gdn_mm=F32)
    win_p = min(MAX_WINDOW, sp)
    return (y_p, y_s, conv_p, delta_p, k_p[:, sp - win_p:], v_p[:, sp - win_p:], conv_s, delta_s, k_s, v_s)
```

```python
import functools
import math

import jax
import jax.numpy as jnp
import numpy as np
from jax import lax
from jax.experimental import pallas as pl
from jax.experimental.pallas import tpu as pltpu

F32 = jnp.float32
BF16 = jnp.bfloat16

D_MODEL = 1024
DEPTH = 4
N_A_LAYERS = 2
EPS = 1e-6

A_HEADS = 8
A_DK = 128
A_DV = 128
CONV_W = 4
A_QK_DIM = A_HEADS * A_DK
A_V_DIM = A_HEADS * A_DV
A_CONV_DIM = 2 * A_QK_DIM + A_V_DIM
A_MAIN_DIM = A_CONV_DIM + A_V_DIM
GATE_PAD = 128

B_GROUPS = ((128, 1), (512, 4), (2048, 16))
N_GROUPS = 3
B_HEADS = 16
B_DH = 64
B_KV_HEADS = 4
B_QPK = B_HEADS // B_KV_HEADS
B_BLOCK = 128
B_KV_DIM = B_KV_HEADS * B_DH
B_Q_DIM = B_HEADS * B_DH
MAX_WINDOW = 2048

D_FF = 2816
FF_CHUNK = 256

LANES = 128
NEG = -1e30
V7X_VMEM_LIMIT = 56 * 1024 * 1024
HIGHEST = lax.Precision.HIGHEST


def _params(*sem):
    return pltpu.CompilerParams(dimension_semantics=sem, vmem_limit_bytes=V7X_VMEM_LIMIT)


def _resident(shape):
    nd = len(shape)
    return pl.BlockSpec(shape, lambda *_: (0,) * nd, pipeline_mode=pl.Buffered(1))


def _rms(x):
    return x * lax.rsqrt(jnp.mean(x * x, axis=-1, keepdims=True) + EPS)


def _silu(x):
    return x * jax.nn.sigmoid(x)


def _norm_proj_kernel(x_ref, g_ref, *refs, group_of, outs, n_chunk):
    nw = len(group_of)
    w_refs, o_refs = refs[:nw], refs[nw:]
    xs = _rms(x_ref[...])
    normed = {}
    for wi, grp in enumerate(group_of):
        if grp not in normed:
            normed[grp] = (xs * g_ref[grp:grp + 1, :]).astype(BF16)
        xn = normed[grp]
        n = w_refs[wi].shape[1]
        for n0 in range(0, n, n_chunk):
            n1 = min(n0 + n_chunk, n)
            res = jnp.dot(xn, w_refs[wi][:, n0:n1], preferred_element_type=F32)
            for oi, (src, _) in enumerate(outs):
                if src == wi:
                    o_refs[oi][:, n0:n1] = res.astype(o_refs[oi].dtype)


def _norm_proj(x, gains, weights, group_of, outs, tm):
    m, d = x.shape
    tm = min(tm, m)
    kern = functools.partial(_norm_proj_kernel, group_of=tuple(group_of), outs=tuple(outs), n_chunk=512)
    return pl.pallas_call(
        kern,
        grid=(m // tm,),
        in_specs=[pl.BlockSpec((tm, d), lambda i: (i, 0)), _resident(gains.shape)]
        + [_resident(w.shape) for w in weights],
        out_specs=[pl.BlockSpec((tm, weights[src].shape[1]), lambda i: (i, 0)) for src, _ in outs],
        out_shape=[jax.ShapeDtypeStruct((m, weights[src].shape[1]), dt) for src, dt in outs],
        compiler_params=_params("parallel"),
        name="norm_proj",
    )(x, gains, *weights)


def _gdn_kernel(pm_ref, bd_ref, hist_ref, s0_ref, cw_ref, ab_ref, og_ref,
                o_ref, s_ref, hnew_ref, xh_ref, *, bt, chunk, mm_dtype):
    nh, c = A_HEADS, chunk
    bh = bt * nh

    @pl.when(pl.program_id(1) == 0)
    def _start():
        s_ref[...] = s0_ref[...]
        xh_ref[:, 5:8, :] = hist_ref[...]

    x = pm_ref[:, :, 0:A_CONV_DIM]
    xh_ref[:, 8:8 + c, :] = x
    cw = cw_ref[...]
    conv = x * cw[CONV_W - 1:CONV_W, :]
    for j in range(1, CONV_W):
        conv = conv + xh_ref[:, 8 - j:8 - j + c, :] * cw[CONV_W - 1 - j:CONV_W - j, :]
    tail = xh_ref[:, 5 + c:8 + c, :]
    xh_ref[:, 5:8, :] = tail
    hnew_ref[...] = tail
    act = _silu(conv)

    def heads(base):
        return jnp.stack([act[bi, :, base + h * A_DK: base + (h + 1) * A_DK]
                          for bi in range(bt) for h in range(nh)])

    qr, kr, v = heads(0), heads(A_QK_DIM), heads(2 * A_QK_DIM)
    q = qr * (lax.rsqrt(jnp.sum(qr * qr, axis=-1, keepdims=True) + EPS) * (A_DK ** -0.5))
    k = kr * lax.rsqrt(jnp.sum(kr * kr, axis=-1, keepdims=True) + EPS)

    bd = bd_ref[...]
    beta_all = jax.nn.sigmoid(bd)
    z = bd + ab_ref[1:2, :]
    softplus = jnp.maximum(z, 0.0) + jnp.log1p(jnp.exp(-jnp.abs(z)))
    g_all = -jnp.exp(ab_ref[0:1, :]) * softplus

    ti = lax.broadcasted_iota(jnp.int32, (c, c), 0)
    si = lax.broadcasted_iota(jnp.int32, (c, c), 1)
    incl = ti >= si
    strict = ti > si
    tri = incl.astype(F32)
    gcol, grow, bcol, glast = [], [], [], []
    for bi in range(bt):
        gc = jnp.dot(tri, g_all[bi], precision=HIGHEST, preferred_element_type=F32)
        gct = gc.T
        for h in range(nh):
            gcol.append(gc[:, nh + h:nh + h + 1])
            grow.append(gct[nh + h:nh + h + 1, :])
            glast.append(gc[c - 1:c, nh + h:nh + h + 1])
            bcol.append(beta_all[bi][:, h:h + 1])
    gcol, grow, bcol, glast = jnp.stack(gcol), jnp.stack(grow), jnp.stack(bcol), jnp.stack(glast)

    decay = jnp.exp(jnp.where(incl, gcol - grow, NEG))
    gam = jnp.exp(gcol)

    def bmm(a, b):
        return jnp.einsum("bij,bjk->bik", a.astype(mm_dtype), b.astype(mm_dtype),
                          preferred_element_type=F32)

    def bmm_nt(a, b):
        return jnp.einsum("bid,bjd->bij", a.astype(mm_dtype), b.astype(mm_dtype),
                          preferred_element_type=F32)

    kk = bmm_nt(k, k)
    qk = bmm_nt(q, k)
    def bmm_split(a, b):
        a_hi = a.astype(BF16)
        a_lo = (a - a_hi.astype(F32)).astype(BF16)
        b_hi = b.astype(BF16)
        b_lo = (b - b_hi.astype(F32)).astype(BF16)
        return bmm(a_hi, b_hi) + bmm(a_hi, b_lo) + bmm(a_lo, b_hi)

    n_mat = jnp.where(strict, -(bcol * decay * kk), 0.0)
    m_pow = q_inv = n_mat
    for _ in range(int(math.log2(c)) - 1):
        m_pow = bmm(m_pow, m_pow)
        q_inv = q_inv + m_pow + bmm(q_inv, m_pow)
    if mm_dtype == BF16:
        resid = n_mat - q_inv + bmm_split(n_mat, q_inv)
        q_inv = q_inv + resid + bmm(q_inv, resid)
    rhs = jnp.concatenate([(bcol * gam) * k, bcol * v], axis=-1)
    sol = rhs + bmm(q_inv, rhs)
    w_mat, u_base = sol[..., :A_DK], sol[..., A_DK:]

    s_old = s_ref[...].reshape(bh, A_DK, A_DV)
    u = u_base - bmm(w_mat, s_old)
    o = bmm(gam * q, s_old) + bmm(decay * qk, u)
    k_dec = jnp.exp(glast - gcol) * k
    s_new = jnp.exp(glast) * s_old + bmm(jnp.swapaxes(k_dec, 1, 2), u)
    s_ref[...] = s_new.reshape(bt, nh, A_DK, A_DV)

    on = _rms(o) * og_ref[...]
    for bi in range(bt):
        for h in range(nh):
            gate = pm_ref[bi, :, A_CONV_DIM + h * A_DV:A_CONV_DIM + (h + 1) * A_DV]
            o_ref[bi, :, h * A_DV:(h + 1) * A_DV] = (on[bi * nh + h] * _silu(gate)).astype(o_ref.dtype)


def _gdn(pm, bd, hist, s0, layer, conv_w, ab, o_gain, bt, chunk, mm_dtype):
    b, t, _ = pm.shape
    bt = math.gcd(bt, b)
    kern = functools.partial(_gdn_kernel, bt=bt, chunk=chunk, mm_dtype=mm_dtype)
    return pl.pallas_call(
        kern,
        grid=(b // bt, t // chunk),
        in_specs=[
            pl.BlockSpec((bt, chunk, A_MAIN_DIM), lambda i, j: (i, j, 0)),
            pl.BlockSpec((bt, chunk, GATE_PAD), lambda i, j: (i, j, 0)),
            pl.BlockSpec((None, bt, CONV_W - 1, A_CONV_DIM), lambda i, j: (layer, i, 0, 0)),
            pl.BlockSpec((None, bt, A_HEADS, A_DK, A_DV), lambda i, j: (layer, i, 0, 0, 0)),
            pl.BlockSpec((CONV_W, A_CONV_DIM), lambda i, j: (0, 0)),
            pl.BlockSpec((2, GATE_PAD), lambda i, j: (0, 0)),
            pl.BlockSpec((1, A_DV), lambda i, j: (0, 0)),
        ],
        out_specs=[
            pl.BlockSpec((bt, chunk, A_V_DIM), lambda i, j: (i, j, 0)),
            pl.BlockSpec((bt, A_HEADS, A_DK, A_DV), lambda i, j: (i, 0, 0, 0)),
            pl.BlockSpec((bt, CONV_W - 1, A_CONV_DIM), lambda i, j: (i, 0, 0)),
        ],
        out_shape=[
            jax.ShapeDtypeStruct((b, t, A_V_DIM), BF16),
            jax.ShapeDtypeStruct((b, A_HEADS, A_DK, A_DV), F32),
            jax.ShapeDtypeStruct((b, CONV_W - 1, A_CONV_DIM), F32),
        ],
        scratch_shapes=[pltpu.VMEM((bt, chunk + 8, A_CONV_DIM), F32)],
        compiler_params=_params("parallel", "arbitrary"),
        name="gdn_mixer",
    )(pm, bd, hist, s0, conv_w, ab, o_gain)


def _mix_groups(a_refs, st_ref, ex_ref, nat_scr, tm):
    st = st_ref[...]
    ms = [st[:, g * 2 * B_HEADS:g * 2 * B_HEADS + B_HEADS] for g in range(N_GROUPS)]
    ls = [st[:, g * 2 * B_HEADS + B_HEADS:(g + 1) * 2 * B_HEADS] for g in range(N_GROUPS)]
    top = jnp.maximum(jnp.maximum(ms[0], ms[1]), ms[2])
    es = [jnp.exp(m - top) for m in ms]
    inv = 1.0 / (es[0] * ls[0] + es[1] * ls[1] + es[2] * ls[2])
    ex = ex_ref[...]
    nq = B_Q_DIM // LANES
    acc = None
    for (_, dil), e, a_ref in zip(B_GROUPS, es, a_refs):
        w = e * inv
        w_hi = w.astype(BF16)
        w_lo = (w - w_hi.astype(F32)).astype(BF16)
        wide = (jnp.dot(w_hi, ex, preferred_element_type=F32)
                + jnp.dot(w_lo, ex, preferred_element_type=F32))
        if dil == 1:
            part = a_ref[0, 0].astype(F32)
        else:
            rows = tm // dil
            for r in range(dil):
                val = a_ref[0, r].astype(F32)
                for c in range(nq):
                    nat_scr[c, pl.ds(r, rows, stride=dil), :] = val[:, c * LANES:(c + 1) * LANES]
            part = jnp.concatenate([nat_scr[c] for c in range(nq)], axis=1)
        term = wide * part
        acc = term if acc is None else acc + term
    return acc


def _post_ffn_kernel(x_ref, *refs, mixed, tm):
    if mixed:
        a_refs, (st_ref, ex_ref, wo_ref, g_ref, wi_ref, w2_ref, out_ref, acc_ref, nat_scr) = refs[:3], refs[3:]
        y_in = _mix_groups(a_refs, st_ref, ex_ref, nat_scr, tm).astype(BF16)
    else:
        y_ref, wo_ref, g_ref, wi_ref, w2_ref, out_ref, acc_ref = refs
        y_in = y_ref[...]
    y = jnp.dot(y_in, wo_ref[...], preferred_element_type=F32)
    x1 = x_ref[...] + _rms(y) * g_ref[0:1, :]
    out_ref[...] = x1
    hn = (_rms(x1) * g_ref[1:2, :]).astype(BF16)
    for j in range(0, D_FF, FF_CHUNK):
        gate = jnp.dot(hn, wi_ref[:, j:j + FF_CHUNK], preferred_element_type=F32)
        up = jnp.dot(hn, wi_ref[:, D_FF + j:D_FF + j + FF_CHUNK], preferred_element_type=F32)
        part = jnp.dot((_silu(gate) * up).astype(BF16), w2_ref[j:j + FF_CHUNK, :],
                       preferred_element_type=F32)
        if j == 0:
            acc_ref[...] = part
        else:
            acc_ref[...] += part
    out_ref[...] += _rms(acc_ref[...]) * g_ref[2:3, :]


def _post_ffn(x, y, w_o, gains, w_in, w_out, tm, seq=None):
    m, d = x.shape
    tm = min(tm, m)
    mixed = seq is not None
    row = lambda n: pl.BlockSpec((tm, n), lambda i: (i, 0))
    scratch = [pltpu.VMEM((tm, d), F32)]
    if mixed:
        accs, stats = y
        tiles = seq // tm
        ex = jnp.asarray(np.repeat(np.eye(B_HEADS, dtype=np.float32), B_DH, axis=1), BF16)
        y_args = [*accs, stats, ex]
        y_specs = [pl.BlockSpec((1, dil, tm // dil, B_Q_DIM), lambda i: (i // tiles, 0, i % tiles, 0))
                   for _, dil in B_GROUPS] + [row(stats.shape[1]), _resident(ex.shape)]
        scratch.append(pltpu.VMEM((B_Q_DIM // LANES, tm, LANES), F32))
    else:
        y_args, y_specs = [y], [row(y.shape[1])]
    return pl.pallas_call(
        functools.partial(_post_ffn_kernel, mixed=mixed, tm=tm),
        grid=(m // tm,),
        in_specs=[row(d)] + y_specs
        + [_resident(w_o.shape), _resident(gains.shape), _resident(w_in.shape), _resident(w_out.shape)],
        out_specs=row(d),
        out_shape=jax.ShapeDtypeStruct((m, d), F32),
        scratch_shapes=scratch,
        compiler_params=_params("parallel"),
        name="post_ffn",
    )(x, *y_args, w_o, gains, w_in, w_out)


def _decimate_rows(src_ref, n_chunks, dil, rows):
    return jnp.concatenate(
        [jnp.concatenate([src_ref[c, pl.ds(r, rows, stride=dil), :] for c in range(n_chunks)], axis=1)
         for r in range(dil)], axis=0)


def _b_proj_kernel(x_ref, g_ref, *refs, with_kv, ts):
    if with_kv:
        wk_ref, wv_ref, wq_ref, k_ref, v_ref = refs[:5]
        kd_refs, vd_refs, qd_refs = refs[5:8], refs[8:11], refs[11:14]
        xn_scr, kv_scr = refs[14:]
    else:
        wq_ref = refs[0]
        qd_refs = refs[1:4]
        (xn_scr,) = refs[4:]
    xs = _rms(x_ref[...])
    q_gain = 1 if with_kv else 0
    xq = xs * g_ref[q_gain:q_gain + 1, :]
    nq = D_MODEL // LANES
    for c in range(nq):
        xn_scr[c] = xq[:, c * LANES:(c + 1) * LANES]
    for gi, (_, dil) in enumerate(B_GROUPS):
        rows = ts // dil
        lhs = (xq if dil == 1 else _decimate_rows(xn_scr, nq, dil, rows)).astype(BF16)
        for n0 in range(0, B_Q_DIM, 512):
            res = jnp.dot(lhs, wq_ref[:, gi * B_Q_DIM + n0:gi * B_Q_DIM + n0 + 512], preferred_element_type=F32)
            qd_refs[gi][0, :, :, n0:n0 + 512] = res.reshape(dil, rows, 512).astype(BF16)
    if with_kv:
        xkv = (xs * g_ref[0:1, :]).astype(BF16)
        nk = B_KV_DIM // LANES
        for w_ref, nat_ref, d_refs in ((wk_ref, k_ref, kd_refs), (wv_ref, v_ref, vd_refs)):
            val = jnp.dot(xkv, w_ref[...], preferred_element_type=F32)
            nat_ref[...] = val
            for c in range(nk):
                kv_scr[c] = val[:, c * LANES:(c + 1) * LANES]
            for gi, (_, dil) in enumerate(B_GROUPS):
                rows = ts // dil
                dec = val if dil == 1 else _decimate_rows(kv_scr, nk, dil, rows)
                d_refs[gi][0] = dec.reshape(dil, rows, B_KV_DIM).astype(BF16)


def _b_proj(x, gains, weights, b, s, with_kv, ts):
    m, d = x.shape
    tiles = s // ts
    dec = lambda n: [pl.BlockSpec((1, dil, ts // dil, n), lambda i: (i // tiles, 0, i % tiles, 0))
                     for _, dil in B_GROUPS]
    dec_shape = lambda n: [jax.ShapeDtypeStruct((b, dil, s // dil, n), BF16) for _, dil in B_GROUPS]
    nat = pl.BlockSpec((ts, B_KV_DIM), lambda i: (i, 0))
    out_specs, out_shape = dec(B_Q_DIM), dec_shape(B_Q_DIM)
    scratch = [pltpu.VMEM((D_MODEL // LANES, ts, LANES), F32)]
    if with_kv:
        out_specs = [nat, nat] + dec(B_KV_DIM) + dec(B_KV_DIM) + out_specs
        out_shape = [jax.ShapeDtypeStruct((m, B_KV_DIM), F32)] * 2 + dec_shape(B_KV_DIM) * 2 + out_shape
        scratch.append(pltpu.VMEM((B_KV_DIM // LANES, ts, LANES), F32))
    return pl.pallas_call(
        functools.partial(_b_proj_kernel, with_kv=with_kv, ts=ts),
        grid=(m // ts,),
        in_specs=[pl.BlockSpec((ts, d), lambda i: (i, 0)), _resident(gains.shape)]
        + [_resident(w.shape) for w in weights],
        out_specs=out_specs,
        out_shape=out_shape,
        scratch_shapes=scratch,
        compiler_params=_params("parallel"),
        name="attn_proj",
    )(x, gains, *weights)


def _alibi_slopes():
    n = N_GROUPS * B_HEADS
    return (2.0 ** (-8.0 * np.arange(1, n + 1) / n)).astype(np.float32).reshape(N_GROUPS, B_HEADS)


def _band_bias(gi):
    _, dil = B_GROUPS[gi]
    qi = np.arange(B_BLOCK)[:, None]
    ci = np.arange(B_BLOCK)[None, :]
    delta = np.where(ci > qi, B_BLOCK + qi - ci, qi - ci).astype(np.float32)
    rest = -_alibi_slopes()[gi][:, None, None] * (dil * delta)[None]
    first = np.where((ci > qi)[None], np.float32(NEG), rest)
    return np.stack([first, rest]).astype(np.float32)


def _band_attn_kernel(q_ref, kp_ref, kc_ref, vp_ref, vc_ref, bias_ref, o_ref, st_ref, *, far_bias):
    blk = pl.program_id(2)
    first = blk == 0
    sel = jnp.minimum(blk, 1)
    qi = lax.broadcasted_iota(jnp.int32, (B_BLOCK, B_BLOCK), 0)
    ci = lax.broadcasted_iota(jnp.int32, (B_BLOCK, B_BLOCK), 1)
    upper = ci > qi
    diag = ci == qi
    q = q_ref[0, 0]
    kp, kc, vp, vc = kp_ref[0, 0], kc_ref[0, 0], vp_ref[0, 0], vc_ref[0, 0]
    nt = (((1,), (1,)), ((), ()))
    for g in range(B_KV_HEADS):
        cols = slice(g * B_DH, (g + 1) * B_DH)
        qg = jnp.concatenate([q[:, (g * B_QPK + p) * B_DH:(g * B_QPK + p + 1) * B_DH]
                              for p in range(B_QPK)], axis=0)
        s_prev = lax.dot_general(qg, kp[:, cols], nt, preferred_element_type=F32)
        s_cur = lax.dot_general(qg, kc[:, cols], nt, preferred_element_type=F32)
        e_up, e_lo = [], []
        for p in range(B_QPK):
            h = g * B_QPK + p
            rows = slice(p * B_BLOCK, (p + 1) * B_BLOCK)
            sp = s_prev[rows]
            s = jnp.where(upper, sp, s_cur[rows]) + bias_ref[sel, h]
            far = (jnp.sum(jnp.where(diag, sp, 0.0), axis=-1, keepdims=True)
                   + jnp.where(first, NEG, far_bias[h]))
            mx = jnp.maximum(jnp.max(s, axis=-1, keepdims=True), far)
            e = jnp.exp(s - mx)
            e_far = jnp.exp(far - mx)
            st_ref[0, 0, :, h:h + 1] = mx
            st_ref[0, 0, :, B_HEADS + h:B_HEADS + h + 1] = jnp.sum(e, axis=-1, keepdims=True) + e_far
            e_up.append(jnp.where(upper, e, jnp.where(diag, e_far, 0.0)).astype(BF16))
            e_lo.append(jnp.where(upper, 0.0, e).astype(BF16))
        pv = (jnp.dot(jnp.concatenate(e_up, axis=0), vp[:, cols], preferred_element_type=F32)
              + jnp.dot(jnp.concatenate(e_lo, axis=0), vc[:, cols], preferred_element_type=F32))
        for p in range(B_QPK):
            h = g * B_QPK + p
            o_ref[0, 0, :, h * B_DH:(h + 1) * B_DH] = pv[p * B_BLOCK:(p + 1) * B_BLOCK].astype(o_ref.dtype)


def _band_attn(qd, kd, vd, gi):
    win, dil = B_GROUPS[gi]
    assert win // dil == B_BLOCK
    b, _, n_dec, _ = qd.shape
    bias = jnp.asarray(_band_bias(gi))
    far_bias = tuple(float(-sl * win) for sl in _alibi_slopes()[gi])
    cur = lambda bi, r, j: (bi, r, j, 0)
    prev = lambda bi, r, j: (bi, r, jnp.maximum(j - 1, 0), 0)
    blk = lambda n, imap: pl.BlockSpec((1, 1, B_BLOCK, n), imap)
    return pl.pallas_call(
        functools.partial(_band_attn_kernel, far_bias=far_bias),
        grid=(b, dil, n_dec // B_BLOCK),
        in_specs=[blk(B_Q_DIM, cur), blk(B_KV_DIM, prev), blk(B_KV_DIM, cur), blk(B_KV_DIM, prev),
                  blk(B_KV_DIM, cur), _resident(bias.shape)],
        out_specs=[blk(B_Q_DIM, cur), blk(2 * B_HEADS, cur)],
        out_shape=[jax.ShapeDtypeStruct((b, dil, n_dec, B_Q_DIM), BF16),
                   jax.ShapeDtypeStruct((b, dil, n_dec, 2 * B_HEADS), F32)],
        compiler_params=_params("parallel", "parallel", "arbitrary"),
        name=f"band_attn_g{gi}",
    )(qd, kd, kd, vd, vd, bias)


def _decode_ranges(t_past):
    return tuple(max(0, t_past - win) // 16 * 16 for win, _ in B_GROUPS)


def _decode_bias(l_new, t_past):
    slopes = _alibi_slopes()
    rows_grp = B_KV_HEADS * B_QPK * l_new
    starts = _decode_ranges(t_past)
    bias_c = [np.full((rows_grp, t_past - st), NEG, np.float32) for st in starts]
    bias_n = np.full((N_GROUPS * rows_grp, l_new), NEG, np.float32)
    for gi, (win, dil) in enumerate(B_GROUPS):
        r = 0
        for g in range(B_KV_HEADS):
            for p in range(B_QPK):
                for l in range(l_new):
                    for pos, tbl, row in ((np.arange(starts[gi], t_past), bias_c[gi], r),
                                          (t_past + np.arange(l_new), bias_n, gi * rows_grp + r)):
                        dist = t_past + l - pos
                        ok = (dist >= 0) & (dist <= win) & (dist % dil == 0)
                        tbl[row] = np.where(ok, -slopes[gi, g * B_QPK + p] * dist.astype(np.float32), NEG)
                    r += 1
    return bias_c, bias_n


def _decode_attn_kernel(q_ref, ck_ref, cv_ref, kn_ref, vn_ref, b0_ref, b1_ref, b2_ref, bn_ref, o_ref,
                        *, l_new, bb, starts):
    rows_g = B_QPK * l_new
    rows_grp = B_KV_HEADS * rows_g
    lane = lax.broadcasted_iota(jnp.int32, (rows_g, B_KV_DIM), 1)
    keep = [(lane >= g * B_DH) & (lane < (g + 1) * B_DH) for g in range(B_KV_HEADS)]
    nt = (((1,), (1,)), ((), ()))
    for bi in range(bb):
        qt = q_ref[bi]
        kn = kn_ref[bi].astype(BF16)
        vn = vn_ref[bi]
        ms, dens, accs = [], [], []
        for gi, b_ref in enumerate((b0_ref, b1_ref, b2_ref)):
            qg = qt[gi * rows_g:(gi + 1) * rows_g]
            qbd = jnp.concatenate([jnp.where(keep[g], qg, 0.0) for g in range(B_KV_HEADS)],
                                  axis=0).astype(BF16)
            s_c = lax.dot_general(qbd, ck_ref[bi, starts[gi]:, :], nt, preferred_element_type=F32) + b_ref[...]
            s_n = (lax.dot_general(qbd, kn, nt, preferred_element_type=F32)
                   + bn_ref[gi * rows_grp:(gi + 1) * rows_grp, :])
            mx = jnp.maximum(jnp.max(s_c, axis=-1, keepdims=True), jnp.max(s_n, axis=-1, keepdims=True))
            e_c = jnp.exp(s_c - mx)
            e_n = jnp.exp(s_n - mx)
            ms.append(mx)
            dens.append(jnp.sum(e_c, axis=-1, keepdims=True) + jnp.sum(e_n, axis=-1, keepdims=True))
            accs.append(jnp.dot(e_c.astype(BF16), cv_ref[bi, starts[gi]:, :], preferred_element_type=F32)
                        + jnp.dot(e_n, vn, preferred_element_type=F32))
        top = jnp.maximum(jnp.maximum(ms[0], ms[1]), ms[2])
        num, tot = None, None
        for i in range(N_GROUPS):
            a = jnp.exp(ms[i] - top)
            num = a * accs[i] if num is None else num + a * accs[i]
            tot = a * dens[i] if tot is None else tot + a * dens[i]
        out = num / tot
        o_ref[bi] = jnp.concatenate([out[g * rows_g:(g + 1) * rows_g, g * B_DH:(g + 1) * B_DH]
                                     for g in range(B_KV_HEADS)], axis=0)


def _decode_attn(q, k_new, v_new, cache_k, cache_v, b, l_new, bb):
    t_past = cache_k.shape[1]
    rows_g = B_QPK * l_new
    qt = q.reshape(b, l_new, N_GROUPS, B_KV_HEADS, B_QPK, B_DH)
    qt = jnp.transpose(qt, (0, 2, 4, 1, 3, 5)).reshape(b, N_GROUPS * rows_g, B_KV_DIM)
    bias_c, bias_n = _decode_bias(l_new, t_past)
    tables = [jnp.asarray(a) for a in (*bias_c, bias_n)]
    seq = lambda n, d: pl.BlockSpec((bb, n, d), lambda i: (i, 0, 0))
    out = pl.pallas_call(
        functools.partial(_decode_attn_kernel, l_new=l_new, bb=bb, starts=_decode_ranges(t_past)),
        grid=(b // bb,),
        in_specs=[seq(N_GROUPS * rows_g, B_KV_DIM), seq(t_past, B_KV_DIM), seq(t_past, B_KV_DIM),
                  seq(l_new, B_KV_DIM), seq(l_new, B_KV_DIM)] + [_resident(a.shape) for a in tables],
        out_specs=seq(B_KV_HEADS * rows_g, B_DH),
        out_shape=jax.ShapeDtypeStruct((b, B_KV_HEADS * rows_g, B_DH), F32),
        compiler_params=_params("parallel"),
        name="decode_attn",
    )(qt, cache_k, cache_v, k_new.reshape(b, l_new, B_KV_DIM), v_new.reshape(b, l_new, B_KV_DIM), *tables)
    out = out.reshape(b, B_HEADS, l_new, B_DH)
    return jnp.transpose(out, (0, 2, 1, 3)).reshape(b * l_new, B_Q_DIM).astype(BF16)


def _token_order(dec, s):
    b, _, _, n = dec.shape
    return jnp.swapaxes(dec, 1, 2).reshape(b * s, n)


def _trunk(x3, conv_hist, delta_s, kv_past, w, *, tm_proj, tm_ffn, gdn_bt, gdn_chunk, gdn_mm):
    b, t, d = x3.shape
    m = b * t
    x = x3.reshape(m, d)
    new_hist, new_delta = [], []
    k_new = v_new = kd = vd = None
    for layer in range(DEPTH):
        if layer < N_A_LAYERS:
            pm, bd = _norm_proj(x, w["norms"][layer, 0:1], [w["a_w_main"][layer], w["a_w_gate"][layer]],
                                (0, 0), ((0, F32), (1, F32)), tm_proj)
            y, s_new, h_new = _gdn(pm.reshape(b, t, A_MAIN_DIM), bd.reshape(b, t, GATE_PAD),
                                   conv_hist, delta_s, layer, w["a_conv_w"][layer], w["a_ab"][layer],
                                   w["a_o_gain"][layer], gdn_bt, gdn_chunk, gdn_mm)
            y = y.reshape(m, A_V_DIM)
            new_hist.append(h_new)
            new_delta.append(s_new)
            w_mix = w["a_w_out"][layer]
        else:
            j = layer - N_A_LAYERS
            q_gain = w["norms"][layer, 0]
            if kv_past is None:
                if j == 0:
                    res = _b_proj(x, jnp.stack([w["kv_norm"], q_gain]), [w["b_w_k"], w["b_w_v"], w["b_w_q"][j]],
                                  b, t, True, tm_ffn)
                    k_new, v_new, kd, vd, qd = res[0], res[1], res[2:5], res[5:8], res[8:11]
                else:
                    qd = _b_proj(x, q_gain[None], [w["b_w_q"][j]], b, t, False, tm_ffn)
                parts = [_band_attn(qd[gi], kd[gi], vd[gi], gi) for gi in range(N_GROUPS)]
                stats = jnp.concatenate([_token_order(p[1], t) for p in parts], axis=1)
                y = ([p[0] for p in parts], stats)
            else:
                if j == 0:
                    k_new, v_new, q = _norm_proj(
                        x, jnp.stack([w["kv_norm"], q_gain]), [w["b_w_k"], w["b_w_v"], w["b_w_q"][j]],
                        (0, 0, 1), ((0, F32), (1, F32), (2, F32)), tm_proj)
                else:
                    (q,) = _norm_proj(x, q_gain[None], [w["b_w_q"][j]], (0,), ((0, F32),), tm_proj)
                y = _decode_attn(q, k_new, v_new, kv_past[0], kv_past[1], b, t, 2)
            w_mix = w["b_w_o"][j]
        x = _post_ffn(x, y, w_mix, w["norms"][layer, 1:4], w["ffn_w_in"][layer], w["ffn_w_out"][layer], tm_ffn,
                      seq=t if isinstance(y, tuple) else None)
    return (x.reshape(b, t, d), jnp.stack(new_hist), jnp.stack(new_delta),
            k_new.reshape(b, t, B_KV_HEADS, B_DH), v_new.reshape(b, t, B_KV_HEADS, B_DH))


def kernel(x_prompt, x_sample, state_conv, state_delta, cache_k, cache_v, norms, kv_norm, a_w_in,
           a_conv_w, a_log, a_dt_bias, a_o_gain, a_w_out, b_w_kv, b_w_q, b_w_o, ffn_w_in, ffn_w_out):
    bp, sp, _ = x_prompt.shape
    gate_w = a_w_in[:, :, A_MAIN_DIM:]
    gate_w = jnp.pad(gate_w, ((0, 0), (0, 0), (0, GATE_PAD - gate_w.shape[-1])))
    ab = jnp.pad(jnp.stack([a_log, a_dt_bias], axis=1), ((0, 0), (0, 0), (A_HEADS, GATE_PAD - 2 * A_HEADS)))
    w_kv = b_w_kv.reshape(D_MODEL, 2, B_KV_DIM)
    w = {
        "norms": norms, "kv_norm": kv_norm,
        "a_w_main": a_w_in[:, :, :A_MAIN_DIM].astype(BF16), "a_w_gate": gate_w.astype(BF16),
        "a_conv_w": a_conv_w, "a_ab": ab, "a_o_gain": a_o_gain.reshape(N_A_LAYERS, 1, A_DV),
        "a_w_out": a_w_out.astype(BF16),
        "b_w_k": w_kv[:, 0].astype(BF16), "b_w_v": w_kv[:, 1].astype(BF16),
        "b_w_q": (b_w_q * (B_DH ** -0.5)).astype(BF16), "b_w_o": b_w_o.astype(BF16),
        "ffn_w_in": ffn_w_in.astype(BF16), "ffn_w_out": ffn_w_out.astype(BF16),
    }
    zero_hist = jnp.zeros((N_A_LAYERS, bp, CONV_W - 1, A_CONV_DIM), x_prompt.dtype)
    zero_delta = jnp.zeros((N_A_LAYERS, bp, A_HEADS, A_DK, A_DV), x_prompt.dtype)
    y_p, conv_p, delta_p, k_p, v_p = _trunk(
        x_prompt, zero_hist, zero_delta, None, w,
        tm_proj=256, tm_ffn=512, gdn_bt=2, gdn_chunk=128, gdn_mm=BF16)
    bs, t_past = cache_k.shape[:2]
    kv_past = tuple(c.astype(BF16).reshape(bs, t_past, B_KV_DIM) for c in (cache_k, cache_v))
    y_s, conv_s, delta_s, k_s, v_s = _trunk(
        x_sample, state_conv, state_delta, kv_past, w,
        tm_proj=256, tm_ffn=512, gdn_bt=8, gdn_chunk=x_sample.shape[1], gdn_mm=F32)
    win_p = min(MAX_WINDOW, sp)
    return (y_p, y_s, conv_p, delta_p, k_p[:, sp - win_p:], v_p[:, sp - win_p:], conv_s, delta_s, k_s, v_s)
```

```python
import functools
import math

import jax
import jax.numpy as jnp
import numpy as np
from jax import lax
from jax.experimental import pallas as pl
from jax.experimental.pallas import tpu as pltpu

F32 = jnp.float32
BF16 = jnp.bfloat16

D_MODEL = 1024
DEPTH = 4
N_A_LAYERS = 2
EPS = 1e-6

A_HEADS = 8
A_DK = 128
A_DV = 128
CONV_W = 4
A_QK_DIM = A_HEADS * A_DK
A_V_DIM = A_HEADS * A_DV
A_CONV_DIM = 2 * A_QK_DIM + A_V_DIM
A_MAIN_DIM = A_CONV_DIM + A_V_DIM
GATE_PAD = 128

B_GROUPS = ((128, 1), (512, 4), (2048, 16))
N_GROUPS = 3
B_HEADS = 16
B_DH = 64
B_KV_HEADS = 4
B_QPK = B_HEADS // B_KV_HEADS
B_BLOCK = 128
B_KV_DIM = B_KV_HEADS * B_DH
B_Q_DIM = B_HEADS * B_DH
MAX_WINDOW = 2048

D_FF = 2816
FF_CHUNK = 256

LANES = 128
NEG = -1e30
V7X_VMEM_LIMIT = 56 * 1024 * 1024
HIGHEST = lax.Precision.HIGHEST


def _params(*sem):
    return pltpu.CompilerParams(dimension_semantics=sem, vmem_limit_bytes=V7X_VMEM_LIMIT)


def _resident(shape):
    nd = len(shape)
    return pl.BlockSpec(shape, lambda *_: (0,) * nd, pipeline_mode=pl.Buffered(1))


def _rms(x):
    return x * lax.rsqrt(jnp.mean(x * x, axis=-1, keepdims=True) + EPS)


def _silu(x):
    return x * jax.nn.sigmoid(x)


def _norm_proj_kernel(x_ref, g_ref, *refs, group_of, outs, n_chunk):
    nw = len(group_of)
    w_refs, o_refs = refs[:nw], refs[nw:]
    xs = _rms(x_ref[...])
    normed = {}
    for wi, grp in enumerate(group_of):
        if grp not in normed:
            normed[grp] = (xs * g_ref[grp:grp + 1, :]).astype(BF16)
        xn = normed[grp]
        n = w_refs[wi].shape[1]
        for n0 in range(0, n, n_chunk):
            n1 = min(n0 + n_chunk, n)
            res = jnp.dot(xn, w_refs[wi][:, n0:n1], preferred_element_type=F32)
            for oi, (src, _) in enumerate(outs):
                if src == wi:
                    o_refs[oi][:, n0:n1] = res.astype(o_refs[oi].dtype)


def _norm_proj(x, gains, weights, group_of, outs, tm):
    m, d = x.shape
    tm = min(tm, m)
    kern = functools.partial(_norm_proj_kernel, group_of=tuple(group_of), outs=tuple(outs), n_chunk=512)
    return pl.pallas_call(
        kern,
        grid=(m // tm,),
        in_specs=[pl.BlockSpec((tm, d), lambda i: (i, 0)), _resident(gains.shape)]
        + [_resident(w.shape) for w in weights],
        out_specs=[pl.BlockSpec((tm, weights[src].shape[1]), lambda i: (i, 0)) for src, _ in outs],
        out_shape=[jax.ShapeDtypeStruct((m, weights[src].shape[1]), dt) for src, dt in outs],
        compiler_params=_params("parallel"),
        name="norm_proj",
    )(x, gains, *weights)


def _gdn_kernel(pm_ref, bd_ref, hist_ref, s0_ref, cw_ref, ab_ref, og_ref,
                o_ref, s_ref, hnew_ref, xh_ref, *, bt, chunk, mm_dtype):
    nh, c = A_HEADS, chunk
    bh = bt * nh

    @pl.when(pl.program_id(1) == 0)
    def _start():
        s_ref[...] = s0_ref[...]
        xh_ref[:, 5:8, :] = hist_ref[...]

    x = pm_ref[:, :, 0:A_CONV_DIM]
    xh_ref[:, 8:8 + c, :] = x
    cw = cw_ref[...]
    conv = x * cw[CONV_W - 1:CONV_W, :]
    for j in range(1, CONV_W):
        conv = conv + xh_ref[:, 8 - j:8 - j + c, :] * cw[CONV_W - 1 - j:CONV_W - j, :]
    tail = xh_ref[:, 5 + c:8 + c, :]
    xh_ref[:, 5:8, :] = tail
    hnew_ref[...] = tail
    act = _silu(conv)

    def heads(base):
        return jnp.stack([act[bi, :, base + h * A_DK: base + (h + 1) * A_DK]
                          for bi in range(bt) for h in range(nh)])

    qr, kr, v = heads(0), heads(A_QK_DIM), heads(2 * A_QK_DIM)
    q = qr * (lax.rsqrt(jnp.sum(qr * qr, axis=-1, keepdims=True) + EPS) * (A_DK ** -0.5))
    k = kr * lax.rsqrt(jnp.sum(kr * kr, axis=-1, keepdims=True) + EPS)

    bd = bd_ref[...]
    beta_all = jax.nn.sigmoid(bd)
    z = bd + ab_ref[1:2, :]
    softplus = jnp.maximum(z, 0.0) + jnp.log1p(jnp.exp(-jnp.abs(z)))
    g_all = -jnp.exp(ab_ref[0:1, :]) * softplus

    ti = lax.broadcasted_iota(jnp.int32, (c, c), 0)
    si = lax.broadcasted_iota(jnp.int32, (c, c), 1)
    incl = ti >= si
    strict = ti > si
    tri = incl.astype(F32)
    gcol, grow, bcol, glast = [], [], [], []
    for bi in range(bt):
        gc = jnp.dot(tri, g_all[bi], precision=HIGHEST, preferred_element_type=F32)
        gct = gc.T
        for h in range(nh):
            gcol.append(gc[:, nh + h:nh + h + 1])
            grow.append(gct[nh + h:nh + h + 1, :])
            glast.append(gc[c - 1:c, nh + h:nh + h + 1])
            bcol.append(beta_all[bi][:, h:h + 1])
    gcol, grow, bcol, glast = jnp.stack(gcol), jnp.stack(grow), jnp.stack(bcol), jnp.stack(glast)

    decay = jnp.exp(jnp.where(incl, gcol - grow, NEG))
    gam = jnp.exp(gcol)

    def bmm(a, b):
        return jnp.einsum("bij,bjk->bik", a.astype(mm_dtype), b.astype(mm_dtype),
                          preferred_element_type=F32)

    def bmm_nt(a, b):
        return jnp.einsum("bid,bjd->bij", a.astype(mm_dtype), b.astype(mm_dtype),
                          preferred_element_type=F32)

    kk = bmm_nt(k, k)
    qk = bmm_nt(q, k)
    def bmm_split(a, b):
        a_hi = a.astype(BF16)
        a_lo = (a - a_hi.astype(F32)).astype(BF16)
        b_hi = b.astype(BF16)
        b_lo = (b - b_hi.astype(F32)).astype(BF16)
        return bmm(a_hi, b_hi) + bmm(a_hi, b_lo) + bmm(a_lo, b_hi)

    n_mat = jnp.where(strict, -(bcol * decay * kk), 0.0)
    m_pow = q_inv = n_mat
    for _ in range(int(math.log2(c)) - 1):
        m_pow = bmm(m_pow, m_pow)
        q_inv = q_inv + m_pow + bmm(q_inv, m_pow)
    if mm_dtype == BF16:
        resid = n_mat - q_inv + bmm_split(n_mat, q_inv)
        q_inv = q_inv + resid + bmm(q_inv, resid)
    rhs = jnp.concatenate([(bcol * gam) * k, bcol * v], axis=-1)
    sol = rhs + bmm(q_inv, rhs)
    w_mat, u_base = sol[..., :A_DK], sol[..., A_DK:]

    s_old = s_ref[...].reshape(bh, A_DK, A_DV)
    u = u_base - bmm(w_mat, s_old)
    o = bmm(gam * q, s_old) + bmm(decay * qk, u)
    k_dec = jnp.exp(glast - gcol) * k
    s_new = jnp.exp(glast) * s_old + bmm(jnp.swapaxes(k_dec, 1, 2), u)
    s_ref[...] = s_new.reshape(bt, nh, A_DK, A_DV)

    on = _rms(o) * og_ref[...]
    for bi in range(bt):
        for h in range(nh):
            gate = pm_ref[bi, :, A_CONV_DIM + h * A_DV:A_CONV_DIM + (h + 1) * A_DV]
            o_ref[bi, :, h * A_DV:(h + 1) * A_DV] = (on[bi * nh + h] * _silu(gate)).astype(o_ref.dtype)


def _gdn(pm, bd, hist, s0, layer, conv_w, ab, o_gain, bt, chunk, mm_dtype):
    b, t, _ = pm.shape
    bt = math.gcd(bt, b)
    kern = functools.partial(_gdn_kernel, bt=bt, chunk=chunk, mm_dtype=mm_dtype)
    return pl.pallas_call(
        kern,
        grid=(b // bt, t // chunk),
        in_specs=[
            pl.BlockSpec((bt, chunk, A_MAIN_DIM), lambda i, j: (i, j, 0)),
            pl.BlockSpec((bt, chunk, GATE_PAD), lambda i, j: (i, j, 0)),
            pl.BlockSpec((None, bt, CONV_W - 1, A_CONV_DIM), lambda i, j: (layer, i, 0, 0)),
            pl.BlockSpec((None, bt, A_HEADS, A_DK, A_DV), lambda i, j: (layer, i, 0, 0, 0)),
            pl.BlockSpec((CONV_W, A_CONV_DIM), lambda i, j: (0, 0)),
            pl.BlockSpec((2, GATE_PAD), lambda i, j: (0, 0)),
            pl.BlockSpec((1, A_DV), lambda i, j: (0, 0)),
        ],
        out_specs=[
            pl.BlockSpec((bt, chunk, A_V_DIM), lambda i, j: (i, j, 0)),
            pl.BlockSpec((bt, A_HEADS, A_DK, A_DV), lambda i, j: (i, 0, 0, 0)),
            pl.BlockSpec((bt, CONV_W - 1, A_CONV_DIM), lambda i, j: (i, 0, 0)),
        ],
        out_shape=[
            jax.ShapeDtypeStruct((b, t, A_V_DIM), BF16),
            jax.ShapeDtypeStruct((b, A_HEADS, A_DK, A_DV), F32),
            jax.ShapeDtypeStruct((b, CONV_W - 1, A_CONV_DIM), F32),
        ],
        scratch_shapes=[pltpu.VMEM((bt, chunk + 8, A_CONV_DIM), F32)],
        compiler_params=_params("parallel", "arbitrary"),
        name="gdn_mixer",
    )(pm, bd, hist, s0, conv_w, ab, o_gain)


def _mix_groups(a_refs, st_ref, ex_ref, nat_scr, tm):
    st = st_ref[...]
    ms = [st[:, g * 2 * B_HEADS:g * 2 * B_HEADS + B_HEADS] for g in range(N_GROUPS)]
    ls = [st[:, g * 2 * B_HEADS + B_HEADS:(g + 1) * 2 * B_HEADS] for g in range(N_GROUPS)]
    top = jnp.maximum(jnp.maximum(ms[0], ms[1]), ms[2])
    es = [jnp.exp(m - top) for m in ms]
    inv = 1.0 / (es[0] * ls[0] + es[1] * ls[1] + es[2] * ls[2])
    ex = ex_ref[...]
    nq = B_Q_DIM // LANES
    acc = None
    for (_, dil), e, a_ref in zip(B_GROUPS, es, a_refs):
        w = e * inv
        w_hi = w.astype(BF16)
        w_lo = (w - w_hi.astype(F32)).astype(BF16)
        wide = (jnp.dot(w_hi, ex, preferred_element_type=F32)
                + jnp.dot(w_lo, ex, preferred_element_type=F32))
        if dil == 1:
            part = a_ref[0, 0].astype(F32)
        else:
            rows = tm // dil
            for r in range(dil):
                val = a_ref[0, r].astype(F32)
                for c in range(nq):
                    nat_scr[c, pl.ds(r, rows, stride=dil), :] = val[:, c * LANES:(c + 1) * LANES]
            part = jnp.concatenate([nat_scr[c] for c in range(nq)], axis=1)
        term = wide * part
        acc = term if acc is None else acc + term
    return acc


def _post_ffn_kernel(x_ref, *refs, mixed, tm):
    if mixed:
        a_refs, (st_ref, ex_ref, wo_ref, g_ref, wi_ref, w2_ref, out_ref, acc_ref, nat_scr) = refs[:3], refs[3:]
        y_in = _mix_groups(a_refs, st_ref, ex_ref, nat_scr, tm).astype(BF16)
    else:
        y_ref, wo_ref, g_ref, wi_ref, w2_ref, out_ref, acc_ref = refs
        y_in = y_ref[...]
    y = jnp.dot(y_in, wo_ref[...], preferred_element_type=F32)
    x1 = x_ref[...] + _rms(y) * g_ref[0:1, :]
    out_ref[...] = x1
    hn = (_rms(x1) * g_ref[1:2, :]).astype(BF16)
    for j in range(0, D_FF, FF_CHUNK):
        gate = jnp.dot(hn, wi_ref[:, j:j + FF_CHUNK], preferred_element_type=F32)
        up = jnp.dot(hn, wi_ref[:, D_FF + j:D_FF + j + FF_CHUNK], preferred_element_type=F32)
        part = jnp.dot((_silu(gate) * up).astype(BF16), w2_ref[j:j + FF_CHUNK, :],
                       preferred_element_type=F32)
        if j == 0:
            acc_ref[...] = part
        else:
            acc_ref[...] += part
    out_ref[...] += _rms(acc_ref[...]) * g_ref[2:3, :]


def _post_ffn(x, y, w_o, gains, w_in, w_out, tm, seq=None):
    m, d = x.shape
    tm = min(tm, m)
    mixed = seq is not None
    row = lambda n: pl.BlockSpec((tm, n), lambda i: (i, 0))
    scratch = [pltpu.VMEM((tm, d), F32)]
    if mixed:
        accs, stats = y
        tiles = seq // tm
        ex = jnp.asarray(np.repeat(np.eye(B_HEADS, dtype=np.float32), B_DH, axis=1), BF16)
        y_args = [*accs, stats, ex]
        y_specs = [pl.BlockSpec((1, dil, tm // dil, B_Q_DIM), lambda i: (i // tiles, 0, i % tiles, 0))
                   for _, dil in B_GROUPS] + [row(stats.shape[1]), _resident(ex.shape)]
        scratch.append(pltpu.VMEM((B_Q_DIM // LANES, tm, LANES), F32))
    else:
        y_args, y_specs = [y], [row(y.shape[1])]
    return pl.pallas_call(
        functools.partial(_post_ffn_kernel, mixed=mixed, tm=tm),
        grid=(m // tm,),
        in_specs=[row(d)] + y_specs
        + [_resident(w_o.shape), _resident(gains.shape), _resident(w_in.shape), _resident(w_out.shape)],
        out_specs=row(d),
        out_shape=jax.ShapeDtypeStruct((m, d), F32),
        scratch_shapes=scratch,
        compiler_params=_params("parallel"),
        name="post_ffn",
    )(x, *y_args, w_o, gains, w_in, w_out)


def _decimate_rows(src_ref, n_chunks, dil, rows):
    return jnp.concatenate(
        [jnp.concatenate([src_ref[c, pl.ds(r, rows, stride=dil), :] for c in range(n_chunks)], axis=1)
         for r in range(dil)], axis=0)


def _b_proj_kernel(x_ref, g_ref, *refs, with_kv, ts):
    if with_kv:
        wk_ref, wv_ref, wq_ref, k_ref, v_ref = refs[:5]
        kd_refs, vd_refs, qd_refs = refs[5:8], refs[8:11], refs[11:14]
        xn_scr, kv_scr = refs[14:]
    else:
        wq_ref = refs[0]
        qd_refs = refs[1:4]
        (xn_scr,) = refs[4:]
    xs = _rms(x_ref[...])
    q_gain = 1 if with_kv else 0
    xq = xs * g_ref[q_gain:q_gain + 1, :]
    nq = D_MODEL // LANES
    for c in range(nq):
        xn_scr[c] = xq[:, c * LANES:(c + 1) * LANES]
    for gi, (_, dil) in enumerate(B_GROUPS):
        rows = ts // dil
        lhs = (xq if dil == 1 else _decimate_rows(xn_scr, nq, dil, rows)).astype(BF16)
        for n0 in range(0, B_Q_DIM, 512):
            res = jnp.dot(lhs, wq_ref[:, gi * B_Q_DIM + n0:gi * B_Q_DIM + n0 + 512], preferred_element_type=F32)
            qd_refs[gi][0, :, :, n0:n0 + 512] = res.reshape(dil, rows, 512).astype(BF16)
    if with_kv:
        xkv = (xs * g_ref[0:1, :]).astype(BF16)
        nk = B_KV_DIM // LANES
        for w_ref, nat_ref, d_refs in ((wk_ref, k_ref, kd_refs), (wv_ref, v_ref, vd_refs)):
            val = jnp.dot(xkv, w_ref[...], preferred_element_type=F32)
            nat_ref[...] = val
            for c in range(nk):
                kv_scr[c] = val[:, c * LANES:(c + 1) * LANES]
            for gi, (_, dil) in enumerate(B_GROUPS):
                rows = ts // dil
                dec = val if dil == 1 else _decimate_rows(kv_scr, nk, dil, rows)
                d_refs[gi][0] = dec.reshape(dil, rows, B_KV_DIM).astype(BF16)


def _b_proj(x, gains, weights, b, s, with_kv, ts):
    m, d = x.shape
    tiles = s // ts
    dec = lambda n: [pl.BlockSpec((1, dil, ts // dil, n), lambda i: (i // tiles, 0, i % tiles, 0))
                     for _, dil in B_GROUPS]
    dec_shape = lambda n: [jax.ShapeDtypeStruct((b, dil, s // dil, n), BF16) for _, dil in B_GROUPS]
    nat = pl.BlockSpec((ts, B_KV_DIM), lambda i: (i, 0))
    out_specs, out_shape = dec(B_Q_DIM), dec_shape(B_Q_DIM)
    scratch = [pltpu.VMEM((D_MODEL // LANES, ts, LANES), F32)]
    if with_kv:
        out_specs = [nat, nat] + dec(B_KV_DIM) + dec(B_KV_DIM) + out_specs
        out_shape = [jax.ShapeDtypeStruct((m, B_KV_DIM), F32)] * 2 + dec_shape(B_KV_DIM) * 2 + out_shape
        scratch.append(pltpu.VMEM((B_KV_DIM // LANES, ts, LANES), F32))
    return pl.pallas_call(
        functools.partial(_b_proj_kernel, with_kv=with_kv, ts=ts),
        grid=(m // ts,),
        in_specs=[pl.BlockSpec((ts, d), lambda i: (i, 0)), _resident(gains.shape)]
        + [_resident(w.shape) for w in weights],
        out_specs=out_specs,
        out_shape=out_shape,
        scratch_shapes=scratch,
        compiler_params=_params("parallel"),
        name="attn_proj",
    )(x, gains, *weights)


def _alibi_slopes():
    n = N_GROUPS * B_HEADS
    return (2.0 ** (-8.0 * np.arange(1, n + 1) / n)).astype(np.float32).reshape(N_GROUPS, B_HEADS)


def _band_bias(gi):
    _, dil = B_GROUPS[gi]
    qi = np.arange(B_BLOCK)[None, :]
    ci = np.arange(B_BLOCK)[:, None]
    delta = np.where(ci > qi, B_BLOCK + qi - ci, qi - ci).astype(np.float32)
    rest = -_alibi_slopes()[gi][:, None, None] * (dil * delta)[None]
    first = np.where((ci > qi)[None], np.float32(NEG), rest)
    return np.stack([first, rest]).astype(np.float32)


def _band_attn_kernel(q_ref, kp_ref, kc_ref, vp_ref, vc_ref, bias_ref, o_ref, st_ref,
                      sp_scr, sc_scr, eu_scr, el_scr, *, far_bias):
    blk = pl.program_id(2)
    first = blk == 0
    sel = jnp.minimum(blk, 1)
    ci = lax.broadcasted_iota(jnp.int32, (B_BLOCK, B_BLOCK), 0)
    qi = lax.broadcasted_iota(jnp.int32, (B_BLOCK, B_BLOCK), 1)
    upper = ci > qi
    diag = ci == qi
    nt = (((1,), (1,)), ((), ()))
    for g in range(B_KV_HEADS):
        cols = slice(g * B_DH, (g + 1) * B_DH)
        qg = jnp.concatenate([q_ref[0, 0, :, (g * B_QPK + p) * B_DH:(g * B_QPK + p + 1) * B_DH]
                              for p in range(B_QPK)], axis=0)
        sp_scr[g] = lax.dot_general(kp_ref[0, 0, :, cols], qg, nt, preferred_element_type=F32)
        sc_scr[g] = lax.dot_general(kc_ref[0, 0, :, cols], qg, nt, preferred_element_type=F32)
    for h in range(B_HEADS):
        g, p = divmod(h, B_QPK)
        lanes = slice(p * B_BLOCK, (p + 1) * B_BLOCK)
        sp = sp_scr[g, :, lanes]
        s = jnp.where(upper, sp, sc_scr[g, :, lanes]) + bias_ref[sel, h]
        far = (jnp.sum(jnp.where(diag, sp, 0.0), axis=0, keepdims=True)
               + jnp.where(first, NEG, far_bias[h]))
        mx = jnp.maximum(jnp.max(s, axis=0, keepdims=True), far)
        e = jnp.exp(s - mx)
        e_far = jnp.exp(far - mx)
        st_ref[0, 0, 0, h:h + 1, :] = mx
        st_ref[0, 0, 0, B_HEADS + h:B_HEADS + h + 1, :] = jnp.sum(e, axis=0, keepdims=True) + e_far
        eu_scr[g, :, lanes] = jnp.where(upper, e, jnp.where(diag, e_far, 0.0)).astype(BF16)
        el_scr[g, :, lanes] = jnp.where(upper, 0.0, e).astype(BF16)
    for g in range(B_KV_HEADS):
        cols = slice(g * B_DH, (g + 1) * B_DH)
        pv_t = (jnp.dot(vp_ref[0, 0, :, cols].T, eu_scr[g], preferred_element_type=F32)
                + jnp.dot(vc_ref[0, 0, :, cols].T, el_scr[g], preferred_element_type=F32))
        pv = pv_t.T
        for p in range(B_QPK):
            h = g * B_QPK + p
            o_ref[0, 0, :, h * B_DH:(h + 1) * B_DH] = pv[p * B_BLOCK:(p + 1) * B_BLOCK].astype(o_ref.dtype)


def _band_attn(qd, kd, vd, gi):
    win, dil = B_GROUPS[gi]
    assert win // dil == B_BLOCK
    b, _, n_dec, _ = qd.shape
    nb = n_dec // B_BLOCK
    bias = jnp.asarray(_band_bias(gi))
    far_bias = tuple(float(-sl * win) for sl in _alibi_slopes()[gi])
    cur = lambda bi, r, j: (bi, r, j, 0)
    prev = lambda bi, r, j: (bi, r, jnp.maximum(j - 1, 0), 0)
    blk = lambda n, imap: pl.BlockSpec((1, 1, B_BLOCK, n), imap)
    tiles = (B_KV_HEADS, B_BLOCK, B_QPK * B_BLOCK)
    return pl.pallas_call(
        functools.partial(_band_attn_kernel, far_bias=far_bias),
        grid=(b, dil, nb),
        in_specs=[blk(B_Q_DIM, cur), blk(B_KV_DIM, prev), blk(B_KV_DIM, cur), blk(B_KV_DIM, prev),
                  blk(B_KV_DIM, cur), _resident(bias.shape)],
        out_specs=[blk(B_Q_DIM, cur),
                   pl.BlockSpec((1, 1, 1, 2 * B_HEADS, B_BLOCK), lambda bi, r, j: (bi, r, j, 0, 0))],
        out_shape=[jax.ShapeDtypeStruct((b, dil, n_dec, B_Q_DIM), BF16),
                   jax.ShapeDtypeStruct((b, dil, nb, 2 * B_HEADS, B_BLOCK), F32)],
        scratch_shapes=[pltpu.VMEM(tiles, F32), pltpu.VMEM(tiles, F32),
                        pltpu.VMEM(tiles, BF16), pltpu.VMEM(tiles, BF16)],
        compiler_params=_params("parallel", "parallel", "arbitrary"),
        name=f"band_attn_g{gi}",
    )(qd, kd, kd, vd, vd, bias)


def _decode_ranges(t_past):
    return tuple(max(0, t_past - win) // 16 * 16 for win, _ in B_GROUPS)


def _decode_bias(l_new, t_past):
    slopes = _alibi_slopes()
    rows_grp = B_KV_HEADS * B_QPK * l_new
    starts = _decode_ranges(t_past)
    bias_c = [np.full((rows_grp, t_past - st), NEG, np.float32) for st in starts]
    bias_n = np.full((N_GROUPS * rows_grp, l_new), NEG, np.float32)
    for gi, (win, dil) in enumerate(B_GROUPS):
        r = 0
        for g in range(B_KV_HEADS):
            for p in range(B_QPK):
                for l in range(l_new):
                    for pos, tbl, row in ((np.arange(starts[gi], t_past), bias_c[gi], r),
                                          (t_past + np.arange(l_new), bias_n, gi * rows_grp + r)):
                        dist = t_past + l - pos
                        ok = (dist >= 0) & (dist <= win) & (dist % dil == 0)
                        tbl[row] = np.where(ok, -slopes[gi, g * B_QPK + p] * dist.astype(np.float32), NEG)
                    r += 1
    return bias_c, bias_n


def _decode_attn_kernel(q_ref, ck_ref, cv_ref, kn_ref, vn_ref, b0_ref, b1_ref, b2_ref, bn_ref, o_ref,
                        *, l_new, bb, starts):
    rows_g = B_QPK * l_new
    rows_grp = B_KV_HEADS * rows_g
    lane = lax.broadcasted_iota(jnp.int32, (rows_g, B_KV_DIM), 1)
    keep = [(lane >= g * B_DH) & (lane < (g + 1) * B_DH) for g in range(B_KV_HEADS)]
    nt = (((1,), (1,)), ((), ()))
    for bi in range(bb):
        qt = q_ref[bi]
        kn = kn_ref[bi].astype(BF16)
        vn = vn_ref[bi]
        ms, dens, accs = [], [], []
        for gi, b_ref in enumerate((b0_ref, b1_ref, b2_ref)):
            qg = qt[gi * rows_g:(gi + 1) * rows_g]
            qbd = jnp.concatenate([jnp.where(keep[g], qg, 0.0) for g in range(B_KV_HEADS)],
                                  axis=0).astype(BF16)
            s_c = lax.dot_general(qbd, ck_ref[bi, starts[gi]:, :], nt, preferred_element_type=F32) + b_ref[...]
            s_n = (lax.dot_general(qbd, kn, nt, preferred_element_type=F32)
                   + bn_ref[gi * rows_grp:(gi + 1) * rows_grp, :])
            mx = jnp.maximum(jnp.max(s_c, axis=-1, keepdims=True), jnp.max(s_n, axis=-1, keepdims=True))
            e_c = jnp.exp(s_c - mx)
            e_n = jnp.exp(s_n - mx)
            ms.append(mx)
            dens.append(jnp.sum(e_c, axis=-1, keepdims=True) + jnp.sum(e_n, axis=-1, keepdims=True))
            accs.append(jnp.dot(e_c.astype(BF16), cv_ref[bi, starts[gi]:, :], preferred_element_type=F32)
                        + jnp.dot(e_n, vn, preferred_element_type=F32))
        top = jnp.maximum(jnp.maximum(ms[0], ms[1]), ms[2])
        num, tot = None, None
        for i in range(N_GROUPS):
            a = jnp.exp(ms[i] - top)
            num = a * accs[i] if num is None else num + a * accs[i]
            tot = a * dens[i] if tot is None else tot + a * dens[i]
        out = num / tot
        o_ref[bi] = jnp.concatenate([out[g * rows_g:(g + 1) * rows_g, g * B_DH:(g + 1) * B_DH]
                                     for g in range(B_KV_HEADS)], axis=0)


def _decode_attn(q, k_new, v_new, cache_k, cache_v, b, l_new, bb):
    t_past = cache_k.shape[1]
    rows_g = B_QPK * l_new
    qt = q.reshape(b, l_new, N_GROUPS, B_KV_HEADS, B_QPK, B_DH)
    qt = jnp.transpose(qt, (0, 2, 4, 1, 3, 5)).reshape(b, N_GROUPS * rows_g, B_KV_DIM)
    bias_c, bias_n = _decode_bias(l_new, t_past)
    tables = [jnp.asarray(a) for a in (*bias_c, bias_n)]
    seq = lambda n, d: pl.BlockSpec((bb, n, d), lambda i: (i, 0, 0))
    out = pl.pallas_call(
        functools.partial(_decode_attn_kernel, l_new=l_new, bb=bb, starts=_decode_ranges(t_past)),
        grid=(b // bb,),
        in_specs=[seq(N_GROUPS * rows_g, B_KV_DIM), seq(t_past, B_KV_DIM), seq(t_past, B_KV_DIM),
                  seq(l_new, B_KV_DIM), seq(l_new, B_KV_DIM)] + [_resident(a.shape) for a in tables],
        out_specs=seq(B_KV_HEADS * rows_g, B_DH),
        out_shape=jax.ShapeDtypeStruct((b, B_KV_HEADS * rows_g, B_DH), F32),
        compiler_params=_params("parallel"),
        name="decode_attn",
    )(qt, cache_k, cache_v, k_new.reshape(b, l_new, B_KV_DIM), v_new.reshape(b, l_new, B_KV_DIM), *tables)
    out = out.reshape(b, B_HEADS, l_new, B_DH)
    return jnp.transpose(out, (0, 2, 1, 3)).reshape(b * l_new, B_Q_DIM).astype(BF16)


def _stats_token_order(st):
    b, dil, nb, n, blk = st.shape
    return jnp.transpose(st, (0, 2, 4, 1, 3)).reshape(b * dil * nb * blk, n)


def _trunk(x3, conv_hist, delta_s, kv_past, w, *, tm_proj, tm_ffn, gdn_bt, gdn_chunk, gdn_mm):
    b, t, d = x3.shape
    m = b * t
    x = x3.reshape(m, d)
    new_hist, new_delta = [], []
    k_new = v_new = kd = vd = None
    for layer in range(DEPTH):
        if layer < N_A_LAYERS:
            pm, bd = _norm_proj(x, w["norms"][layer, 0:1], [w["a_w_main"][layer], w["a_w_gate"][layer]],
                                (0, 0), ((0, F32), (1, F32)), tm_proj)
            y, s_new, h_new = _gdn(pm.reshape(b, t, A_MAIN_DIM), bd.reshape(b, t, GATE_PAD),
                                   conv_hist, delta_s, layer, w["a_conv_w"][layer], w["a_ab"][layer],
                                   w["a_o_gain"][layer], gdn_bt, gdn_chunk, gdn_mm)
            y = y.reshape(m, A_V_DIM)
            new_hist.append(h_new)
            new_delta.append(s_new)
            w_mix = w["a_w_out"][layer]
        else:
            j = layer - N_A_LAYERS
            q_gain = w["norms"][layer, 0]
            if kv_past is None:
                if j == 0:
                    res = _b_proj(x, jnp.stack([w["kv_norm"], q_gain]), [w["b_w_k"], w["b_w_v"], w["b_w_q"][j]],
                                  b, t, True, tm_ffn)
                    k_new, v_new, kd, vd, qd = res[0], res[1], res[2:5], res[5:8], res[8:11]
                else:
                    qd = _b_proj(x, q_gain[None], [w["b_w_q"][j]], b, t, False, tm_ffn)
                parts = [_band_attn(qd[gi], kd[gi], vd[gi], gi) for gi in range(N_GROUPS)]
                stats = jnp.concatenate([_stats_token_order(p[1]) for p in parts], axis=1)
                y = ([p[0] for p in parts], stats)
            else:
                if j == 0:
                    k_new, v_new, q = _norm_proj(
                        x, jnp.stack([w["kv_norm"], q_gain]), [w["b_w_k"], w["b_w_v"], w["b_w_q"][j]],
                        (0, 0, 1), ((0, F32), (1, F32), (2, F32)), tm_proj)
                else:
                    (q,) = _norm_proj(x, q_gain[None], [w["b_w_q"][j]], (0,), ((0, F32),), tm_proj)
                y = _decode_attn(q, k_new, v_new, kv_past[0], kv_past[1], b, t, 2)
            w_mix = w["b_w_o"][j]
        x = _post_ffn(x, y, w_mix, w["norms"][layer, 1:4], w["ffn_w_in"][layer], w["ffn_w_out"][layer], tm_ffn,
                      seq=t if isinstance(y, tuple) else None)
    return (x.reshape(b, t, d), jnp.stack(new_hist), jnp.stack(new_delta),
            k_new.reshape(b, t, B_KV_HEADS, B_DH), v_new.reshape(b, t, B_KV_HEADS, B_DH))


def kernel(x_prompt, x_sample, state_conv, state_delta, cache_k, cache_v, norms, kv_norm, a_w_in,
           a_conv_w, a_log, a_dt_bias, a_o_gain, a_w_out, b_w_kv, b_w_q, b_w_o, ffn_w_in, ffn_w_out):
    bp, sp, _ = x_prompt.shape
    gate_w = a_w_in[:, :, A_MAIN_DIM:]
    gate_w = jnp.pad(gate_w, ((0, 0), (0, 0), (0, GATE_PAD - gate_w.shape[-1])))
    ab = jnp.pad(jnp.stack([a_log, a_dt_bias], axis=1), ((0, 0), (0, 0), (A_HEADS, GATE_PAD - 2 * A_HEADS)))
    w_kv = b_w_kv.reshape(D_MODEL, 2, B_KV_DIM)
    w = {
        "norms": norms, "kv_norm": kv_norm,
        "a_w_main": a_w_in[:, :, :A_MAIN_DIM].astype(BF16), "a_w_gate": gate_w.astype(BF16),
        "a_conv_w": a_conv_w, "a_ab": ab, "a_o_gain": a_o_gain.reshape(N_A_LAYERS, 1, A_DV),
        "a_w_out": a_w_out.astype(BF16),
        "b_w_k": w_kv[:, 0].astype(BF16), "b_w_v": w_kv[:, 1].astype(BF16),
        "b_w_q": (b_w_q * (B_DH ** -0.5)).astype(BF16), "b_w_o": b_w_o.astype(BF16),
        "ffn_w_in": ffn_w_in.astype(BF16), "ffn_w_out": ffn_w_out.astype(BF16),
    }
    zero_hist = jnp.zeros((N_A_LAYERS, bp, CONV_W - 1, A_CONV_DIM), x_prompt.dtype)
    zero_delta = jnp.zeros((N_A_LAYERS, bp, A_HEADS, A_DK, A_DV), x_prompt.dtype)
    y_p, conv_p, delta_p, k_p, v_p = _trunk(
        x_prompt, zero_hist, zero_delta, None, w,
        tm_proj=256, tm_ffn=512, gdn_bt=2, gdn_chunk=128, gdn_mm=BF16)
    bs, t_past = cache_k.shape[:2]
    kv_past = tuple(c.astype(BF16).reshape(bs, t_past, B_KV_DIM) for c in (cache_k, cache_v))
    y_s, conv_s, delta_s, k_s, v_s = _trunk(
        x_sample, state_conv, state_delta, kv_past, w,
        tm_proj=256, tm_ffn=512, gdn_bt=8, gdn_chunk=x_sample.shape[1], gdn_mm=F32)
    win_p = min(MAX_WINDOW, sp)
    return (y_p, y_s, conv_p, delta_p, k_p[:, sp - win_p:], v_p[:, sp - win_p:], conv_s, delta_s, k_s, v_s)
```

```python
import functools
import math

import jax
import jax.numpy as jnp
import numpy as np
from jax import lax
from jax.experimental import pallas as pl
from jax.experimental.pallas import tpu as pltpu

F32 = jnp.float32
BF16 = jnp.bfloat16

D_MODEL = 1024
DEPTH = 4
N_A_LAYERS = 2
EPS = 1e-6

A_HEADS = 8
A_DK = 128
A_DV = 128
CONV_W = 4
A_QK_DIM = A_HEADS * A_DK
A_V_DIM = A_HEADS * A_DV
A_CONV_DIM = 2 * A_QK_DIM + A_V_DIM
A_MAIN_DIM = A_CONV_DIM + A_V_DIM
GATE_PAD = 128

B_GROUPS = ((128, 1), (512, 4), (2048, 16))
N_GROUPS = 3
B_HEADS = 16
B_DH = 64
B_KV_HEADS = 4
B_QPK = B_HEADS // B_KV_HEADS
B_BLOCK = 128
B_KV_DIM = B_KV_HEADS * B_DH
B_Q_DIM = B_HEADS * B_DH
MAX_WINDOW = 2048

D_FF = 2816
FF_CHUNK = 256

LANES = 128
NEG = -1e30
V7X_VMEM_LIMIT = 56 * 1024 * 1024
HIGHEST = lax.Precision.HIGHEST


def _params(*sem):
    return pltpu.CompilerParams(dimension_semantics=sem, vmem_limit_bytes=V7X_VMEM_LIMIT)


def _resident(shape):
    nd = len(shape)
    return pl.BlockSpec(shape, lambda *_: (0,) * nd, pipeline_mode=pl.Buffered(1))


def _rms(x):
    return x * lax.rsqrt(jnp.mean(x * x, axis=-1, keepdims=True) + EPS)


def _silu(x):
    return x * jax.nn.sigmoid(x)


def _norm_proj_kernel(x_ref, g_ref, *refs, group_of, outs, n_chunk):
    nw = len(group_of)
    w_refs, o_refs = refs[:nw], refs[nw:]
    xs = _rms(x_ref[...])
    normed = {}
    for wi, grp in enumerate(group_of):
        if grp not in normed:
            normed[grp] = (xs * g_ref[grp:grp + 1, :]).astype(BF16)
        xn = normed[grp]
        n = w_refs[wi].shape[1]
        for n0 in range(0, n, n_chunk):
            n1 = min(n0 + n_chunk, n)
            res = jnp.dot(xn, w_refs[wi][:, n0:n1], preferred_element_type=F32)
            for oi, (src, _) in enumerate(outs):
                if src == wi:
                    o_refs[oi][:, n0:n1] = res.astype(o_refs[oi].dtype)


def _norm_proj(x, gains, weights, group_of, outs, tm):
    m, d = x.shape
    tm = min(tm, m)
    kern = functools.partial(_norm_proj_kernel, group_of=tuple(group_of), outs=tuple(outs), n_chunk=512)
    return pl.pallas_call(
        kern,
        grid=(m // tm,),
        in_specs=[pl.BlockSpec((tm, d), lambda i: (i, 0)), _resident(gains.shape)]
        + [_resident(w.shape) for w in weights],
        out_specs=[pl.BlockSpec((tm, weights[src].shape[1]), lambda i: (i, 0)) for src, _ in outs],
        out_shape=[jax.ShapeDtypeStruct((m, weights[src].shape[1]), dt) for src, dt in outs],
        compiler_params=_params("parallel"),
        name="norm_proj",
    )(x, gains, *weights)


def _gdn_kernel(pm_ref, bd_ref, hist_ref, s0_ref, cw_ref, ab_ref, og_ref,
                o_ref, s_ref, hnew_ref, xh_ref, *, bt, chunk, mm_dtype):
    nh, c = A_HEADS, chunk
    bh = bt * nh

    @pl.when(pl.program_id(1) == 0)
    def _start():
        s_ref[...] = s0_ref[...]
        xh_ref[:, 5:8, :] = hist_ref[...]

    x = pm_ref[:, :, 0:A_CONV_DIM]
    xh_ref[:, 8:8 + c, :] = x
    cw = cw_ref[...]
    conv = x * cw[CONV_W - 1:CONV_W, :]
    for j in range(1, CONV_W):
        conv = conv + xh_ref[:, 8 - j:8 - j + c, :] * cw[CONV_W - 1 - j:CONV_W - j, :]
    tail = xh_ref[:, 5 + c:8 + c, :]
    xh_ref[:, 5:8, :] = tail
    hnew_ref[...] = tail
    act = _silu(conv)

    def heads(base):
        return jnp.stack([act[bi, :, base + h * A_DK: base + (h + 1) * A_DK]
                          for bi in range(bt) for h in range(nh)])

    qr, kr, v = heads(0), heads(A_QK_DIM), heads(2 * A_QK_DIM)
    q = qr * (lax.rsqrt(jnp.sum(qr * qr, axis=-1, keepdims=True) + EPS) * (A_DK ** -0.5))
    k = kr * lax.rsqrt(jnp.sum(kr * kr, axis=-1, keepdims=True) + EPS)

    bd = bd_ref[...]
    beta_all = jax.nn.sigmoid(bd)
    z = bd + ab_ref[1:2, :]
    softplus = jnp.maximum(z, 0.0) + jnp.log1p(jnp.exp(-jnp.abs(z)))
    g_all = -jnp.exp(ab_ref[0:1, :]) * softplus

    ti = lax.broadcasted_iota(jnp.int32, (c, c), 0)
    si = lax.broadcasted_iota(jnp.int32, (c, c), 1)
    incl = ti >= si
    strict = ti > si
    tri = incl.astype(F32)
    gcol, grow, bcol, glast = [], [], [], []
    for bi in range(bt):
        gc = jnp.dot(tri, g_all[bi], precision=HIGHEST, preferred_element_type=F32)
        gct = gc.T
        for h in range(nh):
            gcol.append(gc[:, nh + h:nh + h + 1])
            grow.append(gct[nh + h:nh + h + 1, :])
            glast.append(gc[c - 1:c, nh + h:nh + h + 1])
            bcol.append(beta_all[bi][:, h:h + 1])
    gcol, grow, bcol, glast = jnp.stack(gcol), jnp.stack(grow), jnp.stack(bcol), jnp.stack(glast)

    decay = jnp.exp(jnp.where(incl, gcol - grow, NEG))
    gam = jnp.exp(gcol)

    def bmm(a, b):
        return jnp.einsum("bij,bjk->bik", a.astype(mm_dtype), b.astype(mm_dtype),
                          preferred_element_type=F32)

    def bmm_nt(a, b):
        return jnp.einsum("bid,bjd->bij", a.astype(mm_dtype), b.astype(mm_dtype),
                          preferred_element_type=F32)

    kk = bmm_nt(k, k)
    qk = bmm_nt(q, k)
    def bmm_split(a, b):
        a_hi = a.astype(BF16)
        a_lo = (a - a_hi.astype(F32)).astype(BF16)
        b_hi = b.astype(BF16)
        b_lo = (b - b_hi.astype(F32)).astype(BF16)
        return bmm(a_hi, b_hi) + bmm(a_hi, b_lo) + bmm(a_lo, b_hi)

    n_mat = jnp.where(strict, -(bcol * decay * kk), 0.0)
    m_pow = q_inv = n_mat
    for _ in range(int(math.log2(c)) - 1):
        m_pow = bmm(m_pow, m_pow)
        q_inv = q_inv + m_pow + bmm(q_inv, m_pow)
    if mm_dtype == BF16:
        resid = n_mat - q_inv + bmm_split(n_mat, q_inv)
        q_inv = q_inv + resid + bmm(q_inv, resid)
    rhs = jnp.concatenate([(bcol * gam) * k, bcol * v], axis=-1)
    sol = rhs + bmm(q_inv, rhs)
    w_mat, u_base = sol[..., :A_DK], sol[..., A_DK:]

    s_old = s_ref[...].reshape(bh, A_DK, A_DV)
    u = u_base - bmm(w_mat, s_old)
    o = bmm(gam * q, s_old) + bmm(decay * qk, u)
    k_dec = jnp.exp(glast - gcol) * k
    s_new = jnp.exp(glast) * s_old + bmm(jnp.swapaxes(k_dec, 1, 2), u)
    s_ref[...] = s_new.reshape(bt, nh, A_DK, A_DV)

    on = _rms(o) * og_ref[...]
    for bi in range(bt):
        for h in range(nh):
            gate = pm_ref[bi, :, A_CONV_DIM + h * A_DV:A_CONV_DIM + (h + 1) * A_DV]
            o_ref[bi, :, h * A_DV:(h + 1) * A_DV] = (on[bi * nh + h] * _silu(gate)).astype(o_ref.dtype)


def _gdn(pm, bd, hist, s0, layer, conv_w, ab, o_gain, bt, chunk, mm_dtype):
    b, t, _ = pm.shape
    bt = math.gcd(bt, b)
    kern = functools.partial(_gdn_kernel, bt=bt, chunk=chunk, mm_dtype=mm_dtype)
    return pl.pallas_call(
        kern,
        grid=(b // bt, t // chunk),
        in_specs=[
            pl.BlockSpec((bt, chunk, A_MAIN_DIM), lambda i, j: (i, j, 0)),
            pl.BlockSpec((bt, chunk, GATE_PAD), lambda i, j: (i, j, 0)),
            pl.BlockSpec((None, bt, CONV_W - 1, A_CONV_DIM), lambda i, j: (layer, i, 0, 0)),
            pl.BlockSpec((None, bt, A_HEADS, A_DK, A_DV), lambda i, j: (layer, i, 0, 0, 0)),
            pl.BlockSpec((CONV_W, A_CONV_DIM), lambda i, j: (0, 0)),
            pl.BlockSpec((2, GATE_PAD), lambda i, j: (0, 0)),
            pl.BlockSpec((1, A_DV), lambda i, j: (0, 0)),
        ],
        out_specs=[
            pl.BlockSpec((bt, chunk, A_V_DIM), lambda i, j: (i, j, 0)),
            pl.BlockSpec((bt, A_HEADS, A_DK, A_DV), lambda i, j: (i, 0, 0, 0)),
            pl.BlockSpec((bt, CONV_W - 1, A_CONV_DIM), lambda i, j: (i, 0, 0)),
        ],
        out_shape=[
            jax.ShapeDtypeStruct((b, t, A_V_DIM), BF16),
            jax.ShapeDtypeStruct((b, A_HEADS, A_DK, A_DV), F32),
            jax.ShapeDtypeStruct((b, CONV_W - 1, A_CONV_DIM), F32),
        ],
        scratch_shapes=[pltpu.VMEM((bt, chunk + 8, A_CONV_DIM), F32)],
        compiler_params=_params("parallel", "arbitrary"),
        name="gdn_mixer",
    )(pm, bd, hist, s0, conv_w, ab, o_gain)


def _mix_groups(a_refs, st_ref, ex_ref, nat_scr, tm):
    st = st_ref[...]
    ms = [st[:, g * 2 * B_HEADS:g * 2 * B_HEADS + B_HEADS] for g in range(N_GROUPS)]
    ls = [st[:, g * 2 * B_HEADS + B_HEADS:(g + 1) * 2 * B_HEADS] for g in range(N_GROUPS)]
    top = jnp.maximum(jnp.maximum(ms[0], ms[1]), ms[2])
    es = [jnp.exp(m - top) for m in ms]
    inv = 1.0 / (es[0] * ls[0] + es[1] * ls[1] + es[2] * ls[2])
    ex = ex_ref[...]
    nq = B_Q_DIM // LANES
    acc = None
    for (_, dil), e, a_ref in zip(B_GROUPS, es, a_refs):
        w = e * inv
        w_hi = w.astype(BF16)
        w_lo = (w - w_hi.astype(F32)).astype(BF16)
        wide = (jnp.dot(w_hi, ex, preferred_element_type=F32)
                + jnp.dot(w_lo, ex, preferred_element_type=F32))
        if dil == 1:
            part = a_ref[0, 0].astype(F32)
        else:
            rows = tm // dil
            for r in range(dil):
                val = a_ref[0, r].astype(F32)
                for c in range(nq):
                    nat_scr[c, pl.ds(r, rows, stride=dil), :] = val[:, c * LANES:(c + 1) * LANES]
            part = jnp.concatenate([nat_scr[c] for c in range(nq)], axis=1)
        term = wide * part
        acc = term if acc is None else acc + term
    return acc


def _post_ffn_kernel(x_ref, *refs, mixed, tm):
    if mixed:
        a_refs, (st_ref, ex_ref, wo_ref, g_ref, wi_ref, w2_ref, out_ref, acc_ref, nat_scr) = refs[:3], refs[3:]
        y_in = _mix_groups(a_refs, st_ref, ex_ref, nat_scr, tm).astype(BF16)
    else:
        y_ref, wo_ref, g_ref, wi_ref, w2_ref, out_ref, acc_ref = refs
        y_in = y_ref[...]
    y = jnp.dot(y_in, wo_ref[...], preferred_element_type=F32)
    x1 = x_ref[...] + _rms(y) * g_ref[0:1, :]
    out_ref[...] = x1
    hn = (_rms(x1) * g_ref[1:2, :]).astype(BF16)
    for j in range(0, D_FF, FF_CHUNK):
        gate = jnp.dot(hn, wi_ref[:, j:j + FF_CHUNK], preferred_element_type=F32)
        up = jnp.dot(hn, wi_ref[:, D_FF + j:D_FF + j + FF_CHUNK], preferred_element_type=F32)
        part = jnp.dot((_silu(gate) * up).astype(BF16), w2_ref[j:j + FF_CHUNK, :],
                       preferred_element_type=F32)
        if j == 0:
            acc_ref[...] = part
        else:
            acc_ref[...] += part
    out_ref[...] += _rms(acc_ref[...]) * g_ref[2:3, :]


def _post_ffn(x, y, w_o, gains, w_in, w_out, tm, seq=None):
    m, d = x.shape
    tm = min(tm, m)
    mixed = seq is not None
    row = lambda n: pl.BlockSpec((tm, n), lambda i: (i, 0))
    scratch = [pltpu.VMEM((tm, d), F32)]
    if mixed:
        accs, stats = y
        tiles = seq // tm
        ex = jnp.asarray(np.repeat(np.eye(B_HEADS, dtype=np.float32), B_DH, axis=1), BF16)
        y_args = [*accs, stats, ex]
        y_specs = [pl.BlockSpec((1, dil, tm // dil, B_Q_DIM), lambda i: (i // tiles, 0, i % tiles, 0))
                   for _, dil in B_GROUPS] + [row(stats.shape[1]), _resident(ex.shape)]
        scratch.append(pltpu.VMEM((B_Q_DIM // LANES, tm, LANES), F32))
    else:
        y_args, y_specs = [y], [row(y.shape[1])]
    return pl.pallas_call(
        functools.partial(_post_ffn_kernel, mixed=mixed, tm=tm),
        grid=(m // tm,),
        in_specs=[row(d)] + y_specs
        + [_resident(w_o.shape), _resident(gains.shape), _resident(w_in.shape), _resident(w_out.shape)],
        out_specs=row(d),
        out_shape=jax.ShapeDtypeStruct((m, d), F32),
        scratch_shapes=scratch,
        compiler_params=_params("parallel"),
        name="post_ffn",
    )(x, *y_args, w_o, gains, w_in, w_out)


def _decimate_rows(src_ref, n_chunks, dil, rows):
    return jnp.concatenate(
        [jnp.concatenate([src_ref[c, pl.ds(r, rows, stride=dil), :] for c in range(n_chunks)], axis=1)
         for r in range(dil)], axis=0)


def _b_proj_kernel(x_ref, g_ref, *refs, with_kv, ts):
    if with_kv:
        wk_ref, wv_ref, wq_ref, k_ref, v_ref = refs[:5]
        kd_refs, vd_refs, qd_refs = refs[5:8], refs[8:11], refs[11:14]
        xn_scr, kv_scr = refs[14:]
    else:
        wq_ref = refs[0]
        qd_refs = refs[1:4]
        (xn_scr,) = refs[4:]
    xs = _rms(x_ref[...])
    q_gain = 1 if with_kv else 0
    xq = xs * g_ref[q_gain:q_gain + 1, :]
    nq = D_MODEL // LANES
    for c in range(nq):
        xn_scr[c] = xq[:, c * LANES:(c + 1) * LANES]
    for gi, (_, dil) in enumerate(B_GROUPS):
        rows = ts // dil
        lhs = (xq if dil == 1 else _decimate_rows(xn_scr, nq, dil, rows)).astype(BF16)
        for n0 in range(0, B_Q_DIM, 512):
            res = jnp.dot(lhs, wq_ref[:, gi * B_Q_DIM + n0:gi * B_Q_DIM + n0 + 512], preferred_element_type=F32)
            qd_refs[gi][0, :, :, n0:n0 + 512] = res.reshape(dil, rows, 512).astype(BF16)
    if with_kv:
        xkv = (xs * g_ref[0:1, :]).astype(BF16)
        nk = B_KV_DIM // LANES
        for w_ref, nat_ref, d_refs in ((wk_ref, k_ref, kd_refs), (wv_ref, v_ref, vd_refs)):
            val = jnp.dot(xkv, w_ref[...], preferred_element_type=F32)
            nat_ref[...] = val
            for c in range(nk):
                kv_scr[c] = val[:, c * LANES:(c + 1) * LANES]
            for gi, (_, dil) in enumerate(B_GROUPS):
                rows = ts // dil
                dec = val if dil == 1 else _decimate_rows(kv_scr, nk, dil, rows)
                d_refs[gi][0] = dec.reshape(dil, rows, B_KV_DIM).astype(BF16)


def _b_proj(x, gains, weights, b, s, with_kv, ts):
    m, d = x.shape
    tiles = s // ts
    dec = lambda n: [pl.BlockSpec((1, dil, ts // dil, n), lambda i: (i // tiles, 0, i % tiles, 0))
                     for _, dil in B_GROUPS]
    dec_shape = lambda n: [jax.ShapeDtypeStruct((b, dil, s // dil, n), BF16) for _, dil in B_GROUPS]
    nat = pl.BlockSpec((ts, B_KV_DIM), lambda i: (i, 0))
    out_specs, out_shape = dec(B_Q_DIM), dec_shape(B_Q_DIM)
    scratch = [pltpu.VMEM((D_MODEL // LANES, ts, LANES), F32)]
    if with_kv:
        out_specs = [nat, nat] + dec(B_KV_DIM) + dec(B_KV_DIM) + out_specs
        out_shape = [jax.ShapeDtypeStruct((m, B_KV_DIM), F32)] * 2 + dec_shape(B_KV_DIM) * 2 + out_shape
        scratch.append(pltpu.VMEM((B_KV_DIM // LANES, ts, LANES), F32))
    return pl.pallas_call(
        functools.partial(_b_proj_kernel, with_kv=with_kv, ts=ts),
        grid=(m // ts,),
        in_specs=[pl.BlockSpec((ts, d), lambda i: (i, 0)), _resident(gains.shape)]
        + [_resident(w.shape) for w in weights],
        out_specs=out_specs,
        out_shape=out_shape,
        scratch_shapes=scratch,
        compiler_params=_params("parallel"),
        name="attn_proj",
    )(x, gains, *weights)


def _alibi_slopes():
    n = N_GROUPS * B_HEADS
    return (2.0 ** (-8.0 * np.arange(1, n + 1) / n)).astype(np.float32).reshape(N_GROUPS, B_HEADS)


def _band_bias(gi):
    _, dil = B_GROUPS[gi]
    qi = np.arange(B_BLOCK)[None, :]
    ci = np.arange(B_BLOCK)[:, None]
    delta = np.where(ci > qi, B_BLOCK + qi - ci, qi - ci).astype(np.float32)
    rest = -_alibi_slopes()[gi][:, None, None] * (dil * delta)[None]
    first = np.where((ci > qi)[None], np.float32(NEG), rest)
    return np.stack([first, rest]).astype(np.float32)


def _band_attn_kernel(q_ref, kp_ref, kc_ref, vp_ref, vc_ref, bias_ref, o_ref, st_ref,
                      sp_scr, sc_scr, eu_scr, el_scr, *, far_bias, bb):
    blk = pl.program_id(2)
    first = blk == 0
    sel = jnp.minimum(blk, 1)
    ci = lax.broadcasted_iota(jnp.int32, (B_BLOCK, B_BLOCK), 0)
    qi = lax.broadcasted_iota(jnp.int32, (B_BLOCK, B_BLOCK), 1)
    upper = ci > qi
    diag = ci == qi
    nt = (((1,), (1,)), ((), ()))
    for bi in range(bb):
        for g in range(B_KV_HEADS):
            cols = slice(g * B_DH, (g + 1) * B_DH)
            slot = bi * B_KV_HEADS + g
            qg = jnp.concatenate([q_ref[bi, 0, :, (g * B_QPK + p) * B_DH:(g * B_QPK + p + 1) * B_DH]
                                  for p in range(B_QPK)], axis=0)
            sp_scr[slot] = lax.dot_general(kp_ref[bi, 0, :, cols], qg, nt, preferred_element_type=F32)
            sc_scr[slot] = lax.dot_general(kc_ref[bi, 0, :, cols], qg, nt, preferred_element_type=F32)
    for bi in range(bb):
        for h in range(B_HEADS):
            g, p = divmod(h, B_QPK)
            slot = bi * B_KV_HEADS + g
            lanes = slice(p * B_BLOCK, (p + 1) * B_BLOCK)
            sp = sp_scr[slot, :, lanes]
            s = jnp.where(upper, sp, sc_scr[slot, :, lanes]) + bias_ref[sel, h]
            far = (jnp.sum(jnp.where(diag, sp, 0.0), axis=0, keepdims=True)
                   + jnp.where(first, NEG, far_bias[h]))
            mx = jnp.maximum(jnp.max(s, axis=0, keepdims=True), far)
            e = jnp.exp(s - mx)
            e_far = jnp.exp(far - mx)
            st_ref[bi, 0, 0, h:h + 1, :] = mx
            st_ref[bi, 0, 0, B_HEADS + h:B_HEADS + h + 1, :] = jnp.sum(e, axis=0, keepdims=True) + e_far
            eu_scr[slot, :, lanes] = jnp.where(upper, e, jnp.where(diag, e_far, 0.0)).astype(BF16)
            el_scr[slot, :, lanes] = jnp.where(upper, 0.0, e).astype(BF16)
    for bi in range(bb):
        for g in range(B_KV_HEADS):
            cols = slice(g * B_DH, (g + 1) * B_DH)
            slot = bi * B_KV_HEADS + g
            pv_t = (jnp.dot(vp_ref[bi, 0, :, cols].T, eu_scr[slot], preferred_element_type=F32)
                    + jnp.dot(vc_ref[bi, 0, :, cols].T, el_scr[slot], preferred_element_type=F32))
            pv = pv_t.T
            for p in range(B_QPK):
                h = g * B_QPK + p
                o_ref[bi, 0, :, h * B_DH:(h + 1) * B_DH] = pv[p * B_BLOCK:(p + 1) * B_BLOCK].astype(o_ref.dtype)


def _band_attn(qd, kd, vd, gi, bb):
    win, dil = B_GROUPS[gi]
    assert win // dil == B_BLOCK
    b, _, n_dec, _ = qd.shape
    nb = n_dec // B_BLOCK
    bias = jnp.asarray(_band_bias(gi))
    far_bias = tuple(float(-sl * win) for sl in _alibi_slopes()[gi])
    cur = lambda bi, r, j: (bi, r, j, 0)
    prev = lambda bi, r, j: (bi, r, jnp.maximum(j - 1, 0), 0)
    bb = math.gcd(bb, b)
    blk = lambda n, imap: pl.BlockSpec((bb, 1, B_BLOCK, n), imap)
    tiles = (bb * B_KV_HEADS, B_BLOCK, B_QPK * B_BLOCK)
    return pl.pallas_call(
        functools.partial(_band_attn_kernel, far_bias=far_bias, bb=bb),
        grid=(b // bb, dil, nb),
        in_specs=[blk(B_Q_DIM, cur), blk(B_KV_DIM, prev), blk(B_KV_DIM, cur), blk(B_KV_DIM, prev),
                  blk(B_KV_DIM, cur), _resident(bias.shape)],
        out_specs=[blk(B_Q_DIM, cur),
                   pl.BlockSpec((bb, 1, 1, 2 * B_HEADS, B_BLOCK), lambda bi, r, j: (bi, r, j, 0, 0))],
        out_shape=[jax.ShapeDtypeStruct((b, dil, n_dec, B_Q_DIM), BF16),
                   jax.ShapeDtypeStruct((b, dil, nb, 2 * B_HEADS, B_BLOCK), F32)],
        scratch_shapes=[pltpu.VMEM(tiles, F32), pltpu.VMEM(tiles, F32),
                        pltpu.VMEM(tiles, BF16), pltpu.VMEM(tiles, BF16)],
        compiler_params=_params("parallel", "parallel", "arbitrary"),
        name=f"band_attn_g{gi}",
    )(qd, kd, kd, vd, vd, bias)


def _decode_rows(t_past, l_new):
    _, dil_far = B_GROUPS[-1]
    assert t_past % dil_far == 0 and l_new <= dil_far
    tail_start = max(0, t_past - max(win for win, _ in B_GROUPS[:-1])) // dil_far * dil_far
    far = np.arange(tail_start).reshape(-1, dil_far)[:, :l_new].reshape(-1)
    return tail_start, np.concatenate([far, np.arange(tail_start, t_past)])


def _decode_ranges(t_past, l_new):
    _, pos = _decode_rows(t_past, l_new)
    return tuple(int(np.searchsorted(pos, max(0, t_past - win))) // 16 * 16 for win, _ in B_GROUPS)


def _decode_bias(l_new, t_past):
    slopes = _alibi_slopes()
    rows_grp = B_KV_HEADS * B_QPK * l_new
    starts = _decode_ranges(t_past, l_new)
    _, kept = _decode_rows(t_past, l_new)
    bias_c = [np.full((rows_grp, len(kept) - st), NEG, np.float32) for st in starts]
    bias_n = np.full((N_GROUPS * rows_grp, l_new), NEG, np.float32)
    for gi, (win, dil) in enumerate(B_GROUPS):
        r = 0
        for g in range(B_KV_HEADS):
            for p in range(B_QPK):
                for l in range(l_new):
                    for pos, tbl, row in ((kept[starts[gi]:], bias_c[gi], r),
                                          (t_past + np.arange(l_new), bias_n, gi * rows_grp + r)):
                        dist = t_past + l - pos
                        ok = (dist >= 0) & (dist <= win) & (dist % dil == 0)
                        tbl[row] = np.where(ok, -slopes[gi, g * B_QPK + p] * dist.astype(np.float32), NEG)
                    r += 1
    return bias_c, bias_n


def _decode_attn_kernel(q_ref, ck_ref, cv_ref, kn_ref, vn_ref, b0_ref, b1_ref, b2_ref, bn_ref, o_ref,
                        *, l_new, bb, starts):
    rows_g = B_QPK * l_new
    rows_grp = B_KV_HEADS * rows_g
    lane = lax.broadcasted_iota(jnp.int32, (rows_g, B_KV_DIM), 1)
    keep = [(lane >= g * B_DH) & (lane < (g + 1) * B_DH) for g in range(B_KV_HEADS)]
    nt = (((1,), (1,)), ((), ()))
    for bi in range(bb):
        qt = q_ref[bi]
        kn = kn_ref[bi].astype(BF16)
        vn = vn_ref[bi]
        ms, dens, accs = [], [], []
        for gi, b_ref in enumerate((b0_ref, b1_ref, b2_ref)):
            qg = qt[gi * rows_g:(gi + 1) * rows_g]
            qbd = jnp.concatenate([jnp.where(keep[g], qg, 0.0) for g in range(B_KV_HEADS)],
                                  axis=0).astype(BF16)
            s_c = lax.dot_general(qbd, ck_ref[bi, starts[gi]:, :], nt, preferred_element_type=F32) + b_ref[...]
            s_n = (lax.dot_general(qbd, kn, nt, preferred_element_type=F32)
                   + bn_ref[gi * rows_grp:(gi + 1) * rows_grp, :])
            mx = jnp.maximum(jnp.max(s_c, axis=-1, keepdims=True), jnp.max(s_n, axis=-1, keepdims=True))
            e_c = jnp.exp(s_c - mx)
            e_n = jnp.exp(s_n - mx)
            ms.append(mx)
            dens.append(jnp.sum(e_c, axis=-1, keepdims=True) + jnp.sum(e_n, axis=-1, keepdims=True))
            accs.append(jnp.dot(e_c.astype(BF16), cv_ref[bi, starts[gi]:, :], preferred_element_type=F32)
                        + jnp.dot(e_n, vn, preferred_element_type=F32))
        top = jnp.maximum(jnp.maximum(ms[0], ms[1]), ms[2])
        num, tot = None, None
        for i in range(N_GROUPS):
            a = jnp.exp(ms[i] - top)
            num = a * accs[i] if num is None else num + a * accs[i]
            tot = a * dens[i] if tot is None else tot + a * dens[i]
        out = num / tot
        o_ref[bi] = jnp.concatenate([out[g * rows_g:(g + 1) * rows_g, g * B_DH:(g + 1) * B_DH]
                                     for g in range(B_KV_HEADS)], axis=0)


def _decode_cache(cache, l_new):
    bs, t_past = cache.shape[:2]
    _, dil_far = B_GROUPS[-1]
    tail_start, kept = _decode_rows(t_past, l_new)
    far = cache[:, :tail_start].reshape(bs, tail_start // dil_far, dil_far, B_KV_HEADS, B_DH)[:, :, :l_new]
    rows = jnp.concatenate([far.reshape(bs, -1, B_KV_HEADS, B_DH), cache[:, tail_start:]], axis=1)
    return rows.astype(BF16).reshape(bs, len(kept), B_KV_DIM)


def _decode_attn(q, k_new, v_new, cache_k, cache_v, t_past, b, l_new, bb):
    n_keep = cache_k.shape[1]
    rows_g = B_QPK * l_new
    qt = q.reshape(b, l_new, N_GROUPS, B_KV_HEADS, B_QPK, B_DH)
    qt = jnp.transpose(qt, (0, 2, 4, 1, 3, 5)).reshape(b, N_GROUPS * rows_g, B_KV_DIM)
    bias_c, bias_n = _decode_bias(l_new, t_past)
    tables = [jnp.asarray(a) for a in (*bias_c, bias_n)]
    seq = lambda n, d: pl.BlockSpec((bb, n, d), lambda i: (i, 0, 0))
    out = pl.pallas_call(
        functools.partial(_decode_attn_kernel, l_new=l_new, bb=bb, starts=_decode_ranges(t_past, l_new)),
        grid=(b // bb,),
        in_specs=[seq(N_GROUPS * rows_g, B_KV_DIM), seq(n_keep, B_KV_DIM), seq(n_keep, B_KV_DIM),
                  seq(l_new, B_KV_DIM), seq(l_new, B_KV_DIM)] + [_resident(a.shape) for a in tables],
        out_specs=seq(B_KV_HEADS * rows_g, B_DH),
        out_shape=jax.ShapeDtypeStruct((b, B_KV_HEADS * rows_g, B_DH), F32),
        compiler_params=_params("parallel"),
        name="decode_attn",
    )(qt, cache_k, cache_v, k_new.reshape(b, l_new, B_KV_DIM), v_new.reshape(b, l_new, B_KV_DIM), *tables)
    out = out.reshape(b, B_HEADS, l_new, B_DH)
    return jnp.transpose(out, (0, 2, 1, 3)).reshape(b * l_new, B_Q_DIM).astype(BF16)


def _stats_token_order(st):
    b, dil, nb, n, blk = st.shape
    return jnp.transpose(st, (0, 2, 4, 1, 3)).reshape(b * dil * nb * blk, n)


def _trunk(x3, conv_hist, delta_s, kv_past, w, *, tm_proj, tm_ffn, gdn_bt, gdn_chunk, gdn_mm):
    b, t, d = x3.shape
    m = b * t
    x = x3.reshape(m, d)
    new_hist, new_delta = [], []
    k_new = v_new = kd = vd = None
    for layer in range(DEPTH):
        if layer < N_A_LAYERS:
            pm, bd = _norm_proj(x, w["norms"][layer, 0:1], [w["a_w_main"][layer], w["a_w_gate"][layer]],
                                (0, 0), ((0, F32), (1, F32)), tm_proj)
            y, s_new, h_new = _gdn(pm.reshape(b, t, A_MAIN_DIM), bd.reshape(b, t, GATE_PAD),
                                   conv_hist, delta_s, layer, w["a_conv_w"][layer], w["a_ab"][layer],
                                   w["a_o_gain"][layer], gdn_bt, gdn_chunk, gdn_mm)
            y = y.reshape(m, A_V_DIM)
            new_hist.append(h_new)
            new_delta.append(s_new)
            w_mix = w["a_w_out"][layer]
        else:
            j = layer - N_A_LAYERS
            q_gain = w["norms"][layer, 0]
            if kv_past is None:
                if j == 0:
                    res = _b_proj(x, jnp.stack([w["kv_norm"], q_gain]), [w["b_w_k"], w["b_w_v"], w["b_w_q"][j]],
                                  b, t, True, tm_ffn)
                    k_new, v_new, kd, vd, qd = res[0], res[1], res[2:5], res[5:8], res[8:11]
                else:
                    qd = _b_proj(x, q_gain[None], [w["b_w_q"][j]], b, t, False, tm_ffn)
                parts = [_band_attn(qd[gi], kd[gi], vd[gi], gi, 2) for gi in range(N_GROUPS)]
                stats = jnp.concatenate([_stats_token_order(p[1]) for p in parts], axis=1)
                y = ([p[0] for p in parts], stats)
            else:
                if j == 0:
                    k_new, v_new, q = _norm_proj(
                        x, jnp.stack([w["kv_norm"], q_gain]), [w["b_w_k"], w["b_w_v"], w["b_w_q"][j]],
                        (0, 0, 1), ((0, F32), (1, F32), (2, F32)), tm_proj)
                else:
                    (q,) = _norm_proj(x, q_gain[None], [w["b_w_q"][j]], (0,), ((0, F32),), tm_proj)
                y = _decode_attn(q, k_new, v_new, *kv_past, b, t, 2)
            w_mix = w["b_w_o"][j]
        x = _post_ffn(x, y, w_mix, w["norms"][layer, 1:4], w["ffn_w_in"][layer], w["ffn_w_out"][layer], tm_ffn,
                      seq=t if isinstance(y, tuple) else None)
    return (x.reshape(b, t, d), jnp.stack(new_hist), jnp.stack(new_delta),
            k_new.reshape(b, t, B_KV_HEADS, B_DH), v_new.reshape(b, t, B_KV_HEADS, B_DH))


def kernel(x_prompt, x_sample, state_conv, state_delta, cache_k, cache_v, norms, kv_norm, a_w_in,
           a_conv_w, a_log, a_dt_bias, a_o_gain, a_w_out, b_w_kv, b_w_q, b_w_o, ffn_w_in, ffn_w_out):
    bp, sp, _ = x_prompt.shape
    gate_w = a_w_in[:, :, A_MAIN_DIM:]
    gate_w = jnp.pad(gate_w, ((0, 0), (0, 0), (0, GATE_PAD - gate_w.shape[-1])))
    ab = jnp.pad(jnp.stack([a_log, a_dt_bias], axis=1), ((0, 0), (0, 0), (A_HEADS, GATE_PAD - 2 * A_HEADS)))
    w_kv = b_w_kv.reshape(D_MODEL, 2, B_KV_DIM)
    w = {
        "norms": norms, "kv_norm": kv_norm,
        "a_w_main": a_w_in[:, :, :A_MAIN_DIM].astype(BF16), "a_w_gate": gate_w.astype(BF16),
        "a_conv_w": a_conv_w, "a_ab": ab, "a_o_gain": a_o_gain.reshape(N_A_LAYERS, 1, A_DV),
        "a_w_out": a_w_out.astype(BF16),
        "b_w_k": w_kv[:, 0].astype(BF16), "b_w_v": w_kv[:, 1].astype(BF16),
        "b_w_q": (b_w_q * (B_DH ** -0.5)).astype(BF16), "b_w_o": b_w_o.astype(BF16),
        "ffn_w_in": ffn_w_in.astype(BF16), "ffn_w_out": ffn_w_out.astype(BF16),
    }
    zero_hist = jnp.zeros((N_A_LAYERS, bp, CONV_W - 1, A_CONV_DIM), x_prompt.dtype)
    zero_delta = jnp.zeros((N_A_LAYERS, bp, A_HEADS, A_DK, A_DV), x_prompt.dtype)
    y_p, conv_p, delta_p, k_p, v_p = _trunk(
        x_prompt, zero_hist, zero_delta, None, w,
        tm_proj=256, tm_ffn=512, gdn_bt=2, gdn_chunk=128, gdn_mm=BF16)
    l_new = x_sample.shape[1]
    kv_past = (_decode_cache(cache_k, l_new), _decode_cache(cache_v, l_new), cache_k.shape[1])
    y_s, conv_s, delta_s, k_s, v_s = _trunk(
        x_sample, state_conv, state_delta, kv_past, w,
        tm_proj=256, tm_ffn=512, gdn_bt=8, gdn_chunk=x_sample.shape[1], gdn_mm=F32)
    win_p = min(MAX_WINDOW, sp)
    return (y_p, y_s, conv_p, delta_p, k_p[:, sp - win_p:], v_p[:, sp - win_p:], conv_s, delta_s, k_s, v_s)
```

```python
import functools
import math

import jax
import jax.numpy as jnp
import numpy as np
from jax import lax
from jax.experimental import pallas as pl
from jax.experimental.pallas import tpu as pltpu

F32 = jnp.float32
BF16 = jnp.bfloat16

D_MODEL = 1024
DEPTH = 4
N_A_LAYERS = 2
EPS = 1e-6

A_HEADS = 8
A_DK = 128
A_DV = 128
CONV_W = 4
A_QK_DIM = A_HEADS * A_DK
A_V_DIM = A_HEADS * A_DV
A_CONV_DIM = 2 * A_QK_DIM + A_V_DIM
A_MAIN_DIM = A_CONV_DIM + A_V_DIM
GATE_PAD = 128

B_GROUPS = ((128, 1), (512, 4), (2048, 16))
N_GROUPS = 3
B_HEADS = 16
B_DH = 64
B_KV_HEADS = 4
B_QPK = B_HEADS // B_KV_HEADS
B_BLOCK = 128
B_KV_DIM = B_KV_HEADS * B_DH
B_Q_DIM = B_HEADS * B_DH
MAX_WINDOW = 2048

D_FF = 2816
FF_CHUNK = 256

LANES = 128
NEG = -1e30
V7X_VMEM_LIMIT = 56 * 1024 * 1024
HIGHEST = lax.Precision.HIGHEST


def _params(*sem):
    return pltpu.CompilerParams(dimension_semantics=sem, vmem_limit_bytes=V7X_VMEM_LIMIT)


def _resident(shape):
    nd = len(shape)
    return pl.BlockSpec(shape, lambda *_: (0,) * nd, pipeline_mode=pl.Buffered(1))


def _rms(x):
    return x * lax.rsqrt(jnp.mean(x * x, axis=-1, keepdims=True) + EPS)


def _silu(x):
    return x * jax.nn.sigmoid(x)


def _norm_proj_kernel(x_ref, g_ref, *refs, group_of, outs, n_chunk):
    nw = len(group_of)
    w_refs, o_refs = refs[:nw], refs[nw:]
    xs = _rms(x_ref[...])
    normed = {}
    for wi, grp in enumerate(group_of):
        if grp not in normed:
            normed[grp] = (xs * g_ref[grp:grp + 1, :]).astype(BF16)
        xn = normed[grp]
        n = w_refs[wi].shape[1]
        for n0 in range(0, n, n_chunk):
            n1 = min(n0 + n_chunk, n)
            res = jnp.dot(xn, w_refs[wi][:, n0:n1], preferred_element_type=F32)
            for oi, (src, _) in enumerate(outs):
                if src == wi:
                    o_refs[oi][:, n0:n1] = res.astype(o_refs[oi].dtype)


def _norm_proj(x, gains, weights, group_of, outs, tm):
    m, d = x.shape
    tm = min(tm, m)
    kern = functools.partial(_norm_proj_kernel, group_of=tuple(group_of), outs=tuple(outs), n_chunk=512)
    return pl.pallas_call(
        kern,
        grid=(m // tm,),
        in_specs=[pl.BlockSpec((tm, d), lambda i: (i, 0)), _resident(gains.shape)]
        + [_resident(w.shape) for w in weights],
        out_specs=[pl.BlockSpec((tm, weights[src].shape[1]), lambda i: (i, 0)) for src, _ in outs],
        out_shape=[jax.ShapeDtypeStruct((m, weights[src].shape[1]), dt) for src, dt in outs],
        compiler_params=_params("parallel"),
        name="norm_proj",
    )(x, gains, *weights)


def _gdn_kernel(pm_ref, bd_ref, hist_ref, s0_ref, cw_ref, ab_ref, og_ref,
                o_ref, s_ref, hnew_ref, xh_ref, *, bt, chunk, mm_dtype):
    nh, c = A_HEADS, chunk
    bh = bt * nh

    @pl.when(pl.program_id(1) == 0)
    def _start():
        s_ref[...] = s0_ref[...]
        xh_ref[:, 5:8, :] = hist_ref[...]

    x = pm_ref[:, :, 0:A_CONV_DIM]
    xh_ref[:, 8:8 + c, :] = x
    cw = cw_ref[...]
    conv = x * cw[CONV_W - 1:CONV_W, :]
    for j in range(1, CONV_W):
        conv = conv + xh_ref[:, 8 - j:8 - j + c, :] * cw[CONV_W - 1 - j:CONV_W - j, :]
    tail = xh_ref[:, 5 + c:8 + c, :]
    xh_ref[:, 5:8, :] = tail
    hnew_ref[...] = tail
    act = _silu(conv)

    def heads(base):
        return jnp.stack([act[bi, :, base + h * A_DK: base + (h + 1) * A_DK]
                          for bi in range(bt) for h in range(nh)])

    qr, kr, v = heads(0), heads(A_QK_DIM), heads(2 * A_QK_DIM)
    q = qr * (lax.rsqrt(jnp.sum(qr * qr, axis=-1, keepdims=True) + EPS) * (A_DK ** -0.5))
    k = kr * lax.rsqrt(jnp.sum(kr * kr, axis=-1, keepdims=True) + EPS)

    bd = bd_ref[...]
    beta_all = jax.nn.sigmoid(bd)
    z = bd + ab_ref[1:2, :]
    softplus = jnp.maximum(z, 0.0) + jnp.log1p(jnp.exp(-jnp.abs(z)))
    g_all = -jnp.exp(ab_ref[0:1, :]) * softplus

    ti = lax.broadcasted_iota(jnp.int32, (c, c), 0)
    si = lax.broadcasted_iota(jnp.int32, (c, c), 1)
    incl = ti >= si
    strict = ti > si
    tri = incl.astype(F32)
    gcol, grow, bcol, glast = [], [], [], []
    for bi in range(bt):
        gc = jnp.dot(tri, g_all[bi], precision=HIGHEST, preferred_element_type=F32)
        gct = gc.T
        for h in range(nh):
            gcol.append(gc[:, nh + h:nh + h + 1])
            grow.append(gct[nh + h:nh + h + 1, :])
            glast.append(gc[c - 1:c, nh + h:nh + h + 1])
            bcol.append(beta_all[bi][:, h:h + 1])
    gcol, grow, bcol, glast = jnp.stack(gcol), jnp.stack(grow), jnp.stack(bcol), jnp.stack(glast)

    decay = jnp.exp(jnp.where(incl, gcol - grow, NEG))
    gam = jnp.exp(gcol)

    def bmm(a, b):
        return jnp.einsum("bij,bjk->bik", a.astype(mm_dtype), b.astype(mm_dtype),
                          preferred_element_type=F32)

    def bmm_nt(a, b):
        return jnp.einsum("bid,bjd->bij", a.astype(mm_dtype), b.astype(mm_dtype),
                          preferred_element_type=F32)

    kk = bmm_nt(k, k)
    qk = bmm_nt(q, k)
    def bmm_split(a, b):
        a_hi = a.astype(BF16)
        a_lo = (a - a_hi.astype(F32)).astype(BF16)
        b_hi = b.astype(BF16)
        b_lo = (b - b_hi.astype(F32)).astype(BF16)
        return bmm(a_hi, b_hi) + bmm(a_hi, b_lo) + bmm(a_lo, b_hi)

    n_mat = jnp.where(strict, -(bcol * decay * kk), 0.0)
    m_pow = q_inv = n_mat
    for _ in range(int(math.log2(c)) - 1):
        m_pow = bmm(m_pow, m_pow)
        q_inv = q_inv + m_pow + bmm(q_inv, m_pow)
    if mm_dtype == BF16:
        resid = n_mat - q_inv + bmm_split(n_mat, q_inv)
        q_inv = q_inv + resid + bmm(q_inv, resid)
    rhs = jnp.concatenate([(bcol * gam) * k, bcol * v], axis=-1)
    sol = rhs + bmm(q_inv, rhs)
    w_mat, u_base = sol[..., :A_DK], sol[..., A_DK:]

    s_old = s_ref[...].reshape(bh, A_DK, A_DV)
    u = u_base - bmm(w_mat, s_old)
    o = bmm(gam * q, s_old) + bmm(decay * qk, u)
    k_dec = jnp.exp(glast - gcol) * k
    s_new = jnp.exp(glast) * s_old + bmm(jnp.swapaxes(k_dec, 1, 2), u)
    s_ref[...] = s_new.reshape(bt, nh, A_DK, A_DV)

    on = _rms(o) * og_ref[...]
    for bi in range(bt):
        for h in range(nh):
            gate = pm_ref[bi, :, A_CONV_DIM + h * A_DV:A_CONV_DIM + (h + 1) * A_DV]
            o_ref[bi, :, h * A_DV:(h + 1) * A_DV] = (on[bi * nh + h] * _silu(gate)).astype(o_ref.dtype)


def _gdn(pm, bd, hist, s0, layer, conv_w, ab, o_gain, bt, chunk, mm_dtype):
    b, t, _ = pm.shape
    bt = math.gcd(bt, b)
    kern = functools.partial(_gdn_kernel, bt=bt, chunk=chunk, mm_dtype=mm_dtype)
    return pl.pallas_call(
        kern,
        grid=(b // bt, t // chunk),
        in_specs=[
            pl.BlockSpec((bt, chunk, A_MAIN_DIM), lambda i, j: (i, j, 0)),
            pl.BlockSpec((bt, chunk, GATE_PAD), lambda i, j: (i, j, 0)),
            pl.BlockSpec((None, bt, CONV_W - 1, A_CONV_DIM), lambda i, j: (layer, i, 0, 0)),
            pl.BlockSpec((None, bt, A_HEADS, A_DK, A_DV), lambda i, j: (layer, i, 0, 0, 0)),
            pl.BlockSpec((CONV_W, A_CONV_DIM), lambda i, j: (0, 0)),
            pl.BlockSpec((2, GATE_PAD), lambda i, j: (0, 0)),
            pl.BlockSpec((1, A_DV), lambda i, j: (0, 0)),
        ],
        out_specs=[
            pl.BlockSpec((bt, chunk, A_V_DIM), lambda i, j: (i, j, 0)),
            pl.BlockSpec((bt, A_HEADS, A_DK, A_DV), lambda i, j: (i, 0, 0, 0)),
            pl.BlockSpec((bt, CONV_W - 1, A_CONV_DIM), lambda i, j: (i, 0, 0)),
        ],
        out_shape=[
            jax.ShapeDtypeStruct((b, t, A_V_DIM), BF16),
            jax.ShapeDtypeStruct((b, A_HEADS, A_DK, A_DV), F32),
            jax.ShapeDtypeStruct((b, CONV_W - 1, A_CONV_DIM), F32),
        ],
        scratch_shapes=[pltpu.VMEM((bt, chunk + 8, A_CONV_DIM), F32)],
        compiler_params=_params("parallel", "arbitrary"),
        name="gdn_mixer",
    )(pm, bd, hist, s0, conv_w, ab, o_gain)


def _mix_groups(a_refs, st_ref, ex_ref, nat_scr, tm):
    st = st_ref[...]
    ms = [st[:, g * 2 * B_HEADS:g * 2 * B_HEADS + B_HEADS] for g in range(N_GROUPS)]
    ls = [st[:, g * 2 * B_HEADS + B_HEADS:(g + 1) * 2 * B_HEADS] for g in range(N_GROUPS)]
    top = jnp.maximum(jnp.maximum(ms[0], ms[1]), ms[2])
    es = [jnp.exp(m - top) for m in ms]
    inv = 1.0 / (es[0] * ls[0] + es[1] * ls[1] + es[2] * ls[2])
    ex = ex_ref[...]
    nq = B_Q_DIM // LANES
    acc = None
    for (_, dil), e, a_ref in zip(B_GROUPS, es, a_refs):
        w = e * inv
        w_hi = w.astype(BF16)
        w_lo = (w - w_hi.astype(F32)).astype(BF16)
        wide = (jnp.dot(w_hi, ex, preferred_element_type=F32)
                + jnp.dot(w_lo, ex, preferred_element_type=F32))
        if dil == 1:
            part = a_ref[0, 0].astype(F32)
        else:
            rows = tm // dil
            for r in range(dil):
                val = a_ref[0, r].astype(F32)
                for c in range(nq):
                    nat_scr[c, pl.ds(r, rows, stride=dil), :] = val[:, c * LANES:(c + 1) * LANES]
            part = jnp.concatenate([nat_scr[c] for c in range(nq)], axis=1)
        term = wide * part
        acc = term if acc is None else acc + term
    return acc


def _post_ffn_kernel(x_ref, *refs, mixed, tm):
    if mixed:
        a_refs, (st_ref, ex_ref, wo_ref, g_ref, wi_ref, w2_ref, out_ref, acc_ref, nat_scr) = refs[:3], refs[3:]
        y_in = _mix_groups(a_refs, st_ref, ex_ref, nat_scr, tm).astype(BF16)
    else:
        y_ref, wo_ref, g_ref, wi_ref, w2_ref, out_ref, acc_ref = refs
        y_in = y_ref[...]
    y = jnp.dot(y_in, wo_ref[...], preferred_element_type=F32)
    x1 = x_ref[...] + _rms(y) * g_ref[0:1, :]
    out_ref[...] = x1
    hn = (_rms(x1) * g_ref[1:2, :]).astype(BF16)
    for j in range(0, D_FF, FF_CHUNK):
        gate = jnp.dot(hn, wi_ref[:, j:j + FF_CHUNK], preferred_element_type=F32)
        up = jnp.dot(hn, wi_ref[:, D_FF + j:D_FF + j + FF_CHUNK], preferred_element_type=F32)
        part = jnp.dot((_silu(gate) * up).astype(BF16), w2_ref[j:j + FF_CHUNK, :],
                       preferred_element_type=F32)
        if j == 0:
            acc_ref[...] = part
        else:
            acc_ref[...] += part
    out_ref[...] += _rms(acc_ref[...]) * g_ref[2:3, :]


def _post_ffn(x, y, w_o, gains, w_in, w_out, tm, seq=None):
    m, d = x.shape
    tm = min(tm, m)
    mixed = seq is not None
    row = lambda n: pl.BlockSpec((tm, n), lambda i: (i, 0))
    scratch = [pltpu.VMEM((tm, d), F32)]
    if mixed:
        accs, stats = y
        tiles = seq // tm
        ex = jnp.asarray(np.repeat(np.eye(B_HEADS, dtype=np.float32), B_DH, axis=1), BF16)
        y_args = [*accs, stats, ex]
        y_specs = [pl.BlockSpec((1, dil, tm // dil, B_Q_DIM), lambda i: (i // tiles, 0, i % tiles, 0))
                   for _, dil in B_GROUPS] + [row(stats.shape[1]), _resident(ex.shape)]
        scratch.append(pltpu.VMEM((B_Q_DIM // LANES, tm, LANES), F32))
    else:
        y_args, y_specs = [y], [row(y.shape[1])]
    return pl.pallas_call(
        functools.partial(_post_ffn_kernel, mixed=mixed, tm=tm),
        grid=(m // tm,),
        in_specs=[row(d)] + y_specs
        + [_resident(w_o.shape), _resident(gains.shape), _resident(w_in.shape), _resident(w_out.shape)],
        out_specs=row(d),
        out_shape=jax.ShapeDtypeStruct((m, d), F32),
        scratch_shapes=scratch,
        compiler_params=_params("parallel"),
        name="post_ffn",
    )(x, *y_args, w_o, gains, w_in, w_out)


def _decimate_rows(src_ref, n_chunks, dil, rows):
    return jnp.concatenate(
        [jnp.concatenate([src_ref[c, pl.ds(r, rows, stride=dil), :] for c in range(n_chunks)], axis=1)
         for r in range(dil)], axis=0)


def _b_proj_kernel(x_ref, g_ref, *refs, with_kv, ts):
    if with_kv:
        wk_ref, wv_ref, wq_ref, k_ref, v_ref = refs[:5]
        kd_refs, vd_refs, qd_refs = refs[5:8], refs[8:11], refs[11:14]
        xn_scr, kv_scr = refs[14:]
    else:
        wq_ref = refs[0]
        qd_refs = refs[1:4]
        (xn_scr,) = refs[4:]
    xs = _rms(x_ref[...])
    q_gain = 1 if with_kv else 0
    xq = xs * g_ref[q_gain:q_gain + 1, :]
    nq = D_MODEL // LANES
    for c in range(nq):
        xn_scr[c] = xq[:, c * LANES:(c + 1) * LANES]
    for gi, (_, dil) in enumerate(B_GROUPS):
        rows = ts // dil
        lhs = (xq if dil == 1 else _decimate_rows(xn_scr, nq, dil, rows)).astype(BF16)
        for n0 in range(0, B_Q_DIM, 512):
            res = jnp.dot(lhs, wq_ref[:, gi * B_Q_DIM + n0:gi * B_Q_DIM + n0 + 512], preferred_element_type=F32)
            qd_refs[gi][0, :, :, n0:n0 + 512] = res.reshape(dil, rows, 512).astype(BF16)
    if with_kv:
        xkv = (xs * g_ref[0:1, :]).astype(BF16)
        nk = B_KV_DIM // LANES
        for w_ref, nat_ref, d_refs in ((wk_ref, k_ref, kd_refs), (wv_ref, v_ref, vd_refs)):
            val = jnp.dot(xkv, w_ref[...], preferred_element_type=F32)
            nat_ref[...] = val
            for c in range(nk):
                kv_scr[c] = val[:, c * LANES:(c + 1) * LANES]
            for gi, (_, dil) in enumerate(B_GROUPS):
                rows = ts // dil
                dec = val if dil == 1 else _decimate_rows(kv_scr, nk, dil, rows)
                d_refs[gi][0] = dec.reshape(dil, rows, B_KV_DIM).astype(BF16)


def _b_proj(x, gains, weights, b, s, with_kv, ts):
    m, d = x.shape
    tiles = s // ts
    dec = lambda n: [pl.BlockSpec((1, dil, ts // dil, n), lambda i: (i // tiles, 0, i % tiles, 0))
                     for _, dil in B_GROUPS]
    dec_shape = lambda n: [jax.ShapeDtypeStruct((b, dil, s // dil, n), BF16) for _, dil in B_GROUPS]
    nat = pl.BlockSpec((ts, B_KV_DIM), lambda i: (i, 0))
    out_specs, out_shape = dec(B_Q_DIM), dec_shape(B_Q_DIM)
    scratch = [pltpu.VMEM((D_MODEL // LANES, ts, LANES), F32)]
    if with_kv:
        out_specs = [nat, nat] + dec(B_KV_DIM) + dec(B_KV_DIM) + out_specs
        out_shape = [jax.ShapeDtypeStruct((m, B_KV_DIM), F32)] * 2 + dec_shape(B_KV_DIM) * 2 + out_shape
        scratch.append(pltpu.VMEM((B_KV_DIM // LANES, ts, LANES), F32))
    return pl.pallas_call(
        functools.partial(_b_proj_kernel, with_kv=with_kv, ts=ts),
        grid=(m // ts,),
        in_specs=[pl.BlockSpec((ts, d), lambda i: (i, 0)), _resident(gains.shape)]
        + [_resident(w.shape) for w in weights],
        out_specs=out_specs,
        out_shape=out_shape,
        scratch_shapes=scratch,
        compiler_params=_params("parallel"),
        name="attn_proj",
    )(x, gains, *weights)


def _alibi_slopes():
    n = N_GROUPS * B_HEADS
    return (2.0 ** (-8.0 * np.arange(1, n + 1) / n)).astype(np.float32).reshape(N_GROUPS, B_HEADS)


def _band_bias(gi):
    _, dil = B_GROUPS[gi]
    qi = np.arange(B_BLOCK)[None, :]
    ci = np.arange(B_BLOCK)[:, None]
    delta = np.where(ci > qi, B_BLOCK + qi - ci, qi - ci).astype(np.float32)
    rest = -_alibi_slopes()[gi][:, None, None] * (dil * delta)[None]
    first = np.where((ci > qi)[None], np.float32(NEG), rest)
    return np.stack([first, rest]).astype(np.float32)


def _band_attn_kernel(q_ref, kp_ref, kc_ref, vp_ref, vc_ref, bias_ref, o_ref, st_ref,
                      sp_scr, sc_scr, eu_scr, el_scr, *, far_bias, bb):
    blk = pl.program_id(2)
    first = blk == 0
    sel = jnp.minimum(blk, 1)
    ci = lax.broadcasted_iota(jnp.int32, (B_BLOCK, B_BLOCK), 0)
    qi = lax.broadcasted_iota(jnp.int32, (B_BLOCK, B_BLOCK), 1)
    upper = ci > qi
    diag = ci == qi
    nt = (((1,), (1,)), ((), ()))
    for bi in range(bb):
        for g in range(B_KV_HEADS):
            cols = slice(g * B_DH, (g + 1) * B_DH)
            slot = bi * B_KV_HEADS + g
            qg = jnp.concatenate([q_ref[bi, 0, :, (g * B_QPK + p) * B_DH:(g * B_QPK + p + 1) * B_DH]
                                  for p in range(B_QPK)], axis=0)
            sp_scr[slot] = lax.dot_general(kp_ref[bi, 0, :, cols], qg, nt, preferred_element_type=F32)
            sc_scr[slot] = lax.dot_general(kc_ref[bi, 0, :, cols], qg, nt, preferred_element_type=F32)
    for bi in range(bb):
        for h in range(B_HEADS):
            g, p = divmod(h, B_QPK)
            slot = bi * B_KV_HEADS + g
            lanes = slice(p * B_BLOCK, (p + 1) * B_BLOCK)
            sp = sp_scr[slot, :, lanes]
            s = jnp.where(upper, sp, sc_scr[slot, :, lanes]) + bias_ref[sel, h]
            far = (jnp.sum(jnp.where(diag, sp, 0.0), axis=0, keepdims=True)
                   + jnp.where(first, NEG, far_bias[h]))
            mx = jnp.maximum(jnp.max(s, axis=0, keepdims=True), far)
            e = jnp.exp(s - mx)
            e_far = jnp.exp(far - mx)
            st_ref[bi, 0, 0, h:h + 1, :] = mx
            st_ref[bi, 0, 0, B_HEADS + h:B_HEADS + h + 1, :] = jnp.sum(e, axis=0, keepdims=True) + e_far
            eu_scr[slot, :, lanes] = jnp.where(upper, e, jnp.where(diag, e_far, 0.0)).astype(BF16)
            el_scr[slot, :, lanes] = jnp.where(upper, 0.0, e).astype(BF16)
    for bi in range(bb):
        for g in range(B_KV_HEADS):
            cols = slice(g * B_DH, (g + 1) * B_DH)
            slot = bi * B_KV_HEADS + g
            pv_t = (jnp.dot(vp_ref[bi, 0, :, cols].T, eu_scr[slot], preferred_element_type=F32)
                    + jnp.dot(vc_ref[bi, 0, :, cols].T, el_scr[slot], preferred_element_type=F32))
            pv = pv_t.T
            for p in range(B_QPK):
                h = g * B_QPK + p
                o_ref[bi, 0, :, h * B_DH:(h + 1) * B_DH] = pv[p * B_BLOCK:(p + 1) * B_BLOCK].astype(o_ref.dtype)


def _band_attn(qd, kd, vd, gi, bb):
    win, dil = B_GROUPS[gi]
    assert win // dil == B_BLOCK
    b, _, n_dec, _ = qd.shape
    nb = n_dec // B_BLOCK
    bias = jnp.asarray(_band_bias(gi))
    far_bias = tuple(float(-sl * win) for sl in _alibi_slopes()[gi])
    cur = lambda bi, r, j: (bi, r, j, 0)
    prev = lambda bi, r, j: (bi, r, jnp.maximum(j - 1, 0), 0)
    bb = math.gcd(bb, b)
    blk = lambda n, imap: pl.BlockSpec((bb, 1, B_BLOCK, n), imap)
    tiles = (bb * B_KV_HEADS, B_BLOCK, B_QPK * B_BLOCK)
    return pl.pallas_call(
        functools.partial(_band_attn_kernel, far_bias=far_bias, bb=bb),
        grid=(b // bb, dil, nb),
        in_specs=[blk(B_Q_DIM, cur), blk(B_KV_DIM, prev), blk(B_KV_DIM, cur), blk(B_KV_DIM, prev),
                  blk(B_KV_DIM, cur), _resident(bias.shape)],
        out_specs=[blk(B_Q_DIM, cur),
                   pl.BlockSpec((bb, 1, 1, 2 * B_HEADS, B_BLOCK), lambda bi, r, j: (bi, r, j, 0, 0))],
        out_shape=[jax.ShapeDtypeStruct((b, dil, n_dec, B_Q_DIM), BF16),
                   jax.ShapeDtypeStruct((b, dil, nb, 2 * B_HEADS, B_BLOCK), F32)],
        scratch_shapes=[pltpu.VMEM(tiles, F32), pltpu.VMEM(tiles, F32),
                        pltpu.VMEM(tiles, BF16), pltpu.VMEM(tiles, BF16)],
        compiler_params=_params("parallel", "parallel", "arbitrary"),
        name=f"band_attn_g{gi}",
    )(qd, kd, kd, vd, vd, bias)


def _decode_rows(t_past, l_new):
    _, dil_far = B_GROUPS[-1]
    assert t_past % dil_far == 0 and l_new <= dil_far
    tail_start = max(0, t_past - max(win for win, _ in B_GROUPS[:-1])) // dil_far * dil_far
    far = np.arange(tail_start).reshape(-1, dil_far)[:, :l_new].reshape(-1)
    return tail_start, np.concatenate([far, np.arange(tail_start, t_past)])


def _decode_ranges(t_past, l_new):
    _, pos = _decode_rows(t_past, l_new)
    return tuple(int(np.searchsorted(pos, max(0, t_past - win))) // 16 * 16 for win, _ in B_GROUPS)


def _decode_bias(l_new, t_past):
    slopes = _alibi_slopes()
    rows_grp = B_KV_HEADS * B_QPK * l_new
    starts = _decode_ranges(t_past, l_new)
    _, kept = _decode_rows(t_past, l_new)
    bias_c = [np.full((rows_grp, len(kept) - st), NEG, np.float32) for st in starts]
    bias_n = np.full((N_GROUPS * rows_grp, l_new), NEG, np.float32)
    for gi, (win, dil) in enumerate(B_GROUPS):
        r = 0
        for g in range(B_KV_HEADS):
            for p in range(B_QPK):
                for l in range(l_new):
                    for pos, tbl, row in ((kept[starts[gi]:], bias_c[gi], r),
                                          (t_past + np.arange(l_new), bias_n, gi * rows_grp + r)):
                        dist = t_past + l - pos
                        ok = (dist >= 0) & (dist <= win) & (dist % dil == 0)
                        tbl[row] = np.where(ok, -slopes[gi, g * B_QPK + p] * dist.astype(np.float32), NEG)
                    r += 1
    return bias_c, bias_n


def _decode_attn_kernel(q_ref, kf_ref, kt_ref, vf_ref, vt_ref, kn_ref, vn_ref, b0_ref, b1_ref, b2_ref, bn_ref,
                        o_ref, *, l_new, bb, starts):
    rows_g = B_QPK * l_new
    rows_grp = B_KV_HEADS * rows_g
    lane = lax.broadcasted_iota(jnp.int32, (rows_g, B_KV_DIM), 1)
    keep = [(lane >= g * B_DH) & (lane < (g + 1) * B_DH) for g in range(B_KV_HEADS)]
    nt = (((1,), (1,)), ((), ()))
    n_far = kf_ref.shape[1] * kf_ref.shape[2]
    n_tail = kt_ref.shape[1] * kt_ref.shape[2]
    for bi in range(bb):
        qt = q_ref[bi]
        k_far, v_far = kf_ref[bi].reshape(n_far, B_KV_DIM), vf_ref[bi].reshape(n_far, B_KV_DIM)
        k_tail, v_tail = kt_ref[bi].reshape(n_tail, B_KV_DIM), vt_ref[bi].reshape(n_tail, B_KV_DIM)
        ms, dens, accs = [], [], []
        for gi, b_ref in enumerate((b0_ref, b1_ref, b2_ref)):
            qg = qt[gi * rows_g:(gi + 1) * rows_g]
            qbd = jnp.concatenate([jnp.where(keep[g], qg, 0.0) for g in range(B_KV_HEADS)], axis=0)
            st = starts[gi]
            if st < n_far:
                parts = [(k_far[st:], v_far[st:], b_ref[:, :n_far - st]), (k_tail, v_tail, b_ref[:, n_far - st:])]
            else:
                parts = [(k_tail[st - n_far:], v_tail[st - n_far:], b_ref[...])]
            parts.append((kn_ref[bi], vn_ref[bi], bn_ref[gi * rows_grp:(gi + 1) * rows_grp, :]))
            scores = [lax.dot_general(qbd, k, nt, preferred_element_type=F32) + bias for k, _, bias in parts]
            mx = functools.reduce(jnp.maximum, [jnp.max(s, axis=-1, keepdims=True) for s in scores])
            es = [jnp.exp(s - mx) for s in scores]
            ms.append(mx)
            dens.append(sum(jnp.sum(e, axis=-1, keepdims=True) for e in es))
            accs.append(sum(jnp.dot(e, v, preferred_element_type=F32)
                            for e, (_, v, _) in zip(es, parts)))
        top = jnp.maximum(jnp.maximum(ms[0], ms[1]), ms[2])
        num, tot = None, None
        for i in range(N_GROUPS):
            a = jnp.exp(ms[i] - top)
            num = a * accs[i] if num is None else num + a * accs[i]
            tot = a * dens[i] if tot is None else tot + a * dens[i]
        out = num / tot
        o_ref[bi] = jnp.concatenate([out[g * rows_g:(g + 1) * rows_g, g * B_DH:(g + 1) * B_DH]
                                     for g in range(B_KV_HEADS)], axis=0)


def _decode_attn(q, k_new, v_new, cache_k, cache_v, b, l_new, bb):
    t_past = cache_k.shape[1]
    _, dil_far = B_GROUPS[-1]
    tail_start, _ = _decode_rows(t_past, l_new)
    far_blocks, tail_blocks = tail_start // dil_far, (t_past - tail_start) // dil_far
    assert far_blocks % tail_blocks == 0 and l_new % 8 == 0
    rows_g = B_QPK * l_new
    qt = q.reshape(b, l_new, N_GROUPS, B_KV_HEADS, B_QPK, B_DH)
    qt = jnp.transpose(qt, (0, 2, 4, 1, 3, 5)).reshape(b, N_GROUPS * rows_g, B_KV_DIM)
    bias_c, bias_n = _decode_bias(l_new, t_past)
    tables = [jnp.asarray(a) for a in (*bias_c, bias_n)]
    seq = lambda n, d: pl.BlockSpec((bb, n, d), lambda i: (i, 0, 0))
    far = pl.BlockSpec((bb, far_blocks, l_new, B_KV_DIM), lambda i: (i, 0, 0, 0))
    tail = pl.BlockSpec((bb, tail_blocks, dil_far, B_KV_DIM), lambda i: (i, far_blocks // tail_blocks, 0, 0))
    ck, cv = (c.reshape(b, t_past // dil_far, dil_far, B_KV_DIM) for c in (cache_k, cache_v))
    out = pl.pallas_call(
        functools.partial(_decode_attn_kernel, l_new=l_new, bb=bb, starts=_decode_ranges(t_past, l_new)),
        grid=(b // bb,),
        in_specs=[seq(N_GROUPS * rows_g, B_KV_DIM), far, tail, far, tail,
                  seq(l_new, B_KV_DIM), seq(l_new, B_KV_DIM)] + [_resident(a.shape) for a in tables],
        out_specs=seq(B_KV_HEADS * rows_g, B_DH),
        out_shape=jax.ShapeDtypeStruct((b, B_KV_HEADS * rows_g, B_DH), F32),
        compiler_params=_params("parallel"),
        name="decode_attn",
    )(qt, ck, ck, cv, cv, k_new.reshape(b, l_new, B_KV_DIM), v_new.reshape(b, l_new, B_KV_DIM), *tables)
    out = out.reshape(b, B_HEADS, l_new, B_DH)
    return jnp.transpose(out, (0, 2, 1, 3)).reshape(b * l_new, B_Q_DIM).astype(BF16)


def _stats_token_order(st):
    b, dil, nb, n, blk = st.shape
    return jnp.transpose(st, (0, 2, 4, 1, 3)).reshape(b * dil * nb * blk, n)


def _trunk(x3, conv_hist, delta_s, kv_past, w, *, tm_proj, tm_ffn, gdn_bt, gdn_chunk, gdn_mm):
    b, t, d = x3.shape
    m = b * t
    x = x3.reshape(m, d)
    new_hist, new_delta = [], []
    k_new = v_new = kd = vd = None
    for layer in range(DEPTH):
        if layer < N_A_LAYERS:
            pm, bd = _norm_proj(x, w["norms"][layer, 0:1], [w["a_w_main"][layer], w["a_w_gate"][layer]],
                                (0, 0), ((0, F32), (1, F32)), tm_proj)
            y, s_new, h_new = _gdn(pm.reshape(b, t, A_MAIN_DIM), bd.reshape(b, t, GATE_PAD),
                                   conv_hist, delta_s, layer, w["a_conv_w"][layer], w["a_ab"][layer],
                                   w["a_o_gain"][layer], gdn_bt, gdn_chunk, gdn_mm)
            y = y.reshape(m, A_V_DIM)
            new_hist.append(h_new)
            new_delta.append(s_new)
            w_mix = w["a_w_out"][layer]
        else:
            j = layer - N_A_LAYERS
            q_gain = w["norms"][layer, 0]
            if kv_past is None:
                if j == 0:
                    res = _b_proj(x, jnp.stack([w["kv_norm"], q_gain]), [w["b_w_k"], w["b_w_v"], w["b_w_q"][j]],
                                  b, t, True, tm_ffn)
                    k_new, v_new, kd, vd, qd = res[0], res[1], res[2:5], res[5:8], res[8:11]
                else:
                    qd = _b_proj(x, q_gain[None], [w["b_w_q"][j]], b, t, False, tm_ffn)
                parts = [_band_attn(qd[gi], kd[gi], vd[gi], gi, 2) for gi in range(N_GROUPS)]
                stats = jnp.concatenate([_stats_token_order(p[1]) for p in parts], axis=1)
                y = ([p[0] for p in parts], stats)
            else:
                if j == 0:
                    k_new, v_new, q = _norm_proj(
                        x, jnp.stack([w["kv_norm"], q_gain]), [w["b_w_k"], w["b_w_v"], w["b_w_q"][j]],
                        (0, 0, 1), ((0, F32), (1, F32), (2, F32)), tm_proj)
                else:
                    (q,) = _norm_proj(x, q_gain[None], [w["b_w_q"][j]], (0,), ((0, F32),), tm_proj)
                y = _decode_attn(q, k_new, v_new, *kv_past, b, t, 2)
            w_mix = w["b_w_o"][j]
        x = _post_ffn(x, y, w_mix, w["norms"][layer, 1:4], w["ffn_w_in"][layer], w["ffn_w_out"][layer], tm_ffn,
                      seq=t if isinstance(y, tuple) else None)
    return (x.reshape(b, t, d), jnp.stack(new_hist), jnp.stack(new_delta),
            k_new.reshape(b, t, B_KV_HEADS, B_DH), v_new.reshape(b, t, B_KV_HEADS, B_DH))


def kernel(x_prompt, x_sample, state_conv, state_delta, cache_k, cache_v, norms, kv_norm, a_w_in,
           a_conv_w, a_log, a_dt_bias, a_o_gain, a_w_out, b_w_kv, b_w_q, b_w_o, ffn_w_in, ffn_w_out):
    bp, sp, _ = x_prompt.shape
    gate_w = a_w_in[:, :, A_MAIN_DIM:]
    gate_w = jnp.pad(gate_w, ((0, 0), (0, 0), (0, GATE_PAD - gate_w.shape[-1])))
    ab = jnp.pad(jnp.stack([a_log, a_dt_bias], axis=1), ((0, 0), (0, 0), (A_HEADS, GATE_PAD - 2 * A_HEADS)))
    w_kv = b_w_kv.reshape(D_MODEL, 2, B_KV_DIM)
    w = {
        "norms": norms, "kv_norm": kv_norm,
        "a_w_main": a_w_in[:, :, :A_MAIN_DIM].astype(BF16), "a_w_gate": gate_w.astype(BF16),
        "a_conv_w": a_conv_w, "a_ab": ab, "a_o_gain": a_o_gain.reshape(N_A_LAYERS, 1, A_DV),
        "a_w_out": a_w_out.astype(BF16),
        "b_w_k": w_kv[:, 0].astype(BF16), "b_w_v": w_kv[:, 1].astype(BF16),
        "b_w_q": (b_w_q * (B_DH ** -0.5)).astype(BF16), "b_w_o": b_w_o.astype(BF16),
        "ffn_w_in": ffn_w_in.astype(BF16), "ffn_w_out": ffn_w_out.astype(BF16),
    }
    zero_hist = jnp.zeros((N_A_LAYERS, bp, CONV_W - 1, A_CONV_DIM), x_prompt.dtype)
    zero_delta = jnp.zeros((N_A_LAYERS, bp, A_HEADS, A_DK, A_DV), x_prompt.dtype)
    y_p, conv_p, delta_p, k_p, v_p = _trunk(
        x_prompt, zero_hist, zero_delta, None, w,
        tm_proj=256, tm_ffn=512, gdn_bt=2, gdn_chunk=128, gdn_mm=BF16)
    bs, t_past = cache_k.shape[:2]
    kv_past = tuple(c.reshape(bs, t_past, B_KV_DIM) for c in (cache_k, cache_v))
    y_s, conv_s, delta_s, k_s, v_s = _trunk(
        x_sample, state_conv, state_delta, kv_past, w,
        tm_proj=256, tm_ffn=512, gdn_bt=8, gdn_chunk=x_sample.shape[1], gdn_mm=F32)
    win_p = min(MAX_WINDOW, sp)
    return (y_p, y_s, conv_p, delta_p, k_p[:, sp - win_p:], v_p[:, sp - win_p:], conv_s, delta_s, k_s, v_s)
```

```python
import functools
import math

import jax
import jax.numpy as jnp
import numpy as np
from jax import lax
from jax.experimental import pallas as pl
from jax.experimental.pallas import tpu as pltpu

F32 = jnp.float32
BF16 = jnp.bfloat16

D_MODEL = 1024
DEPTH = 4
N_A_LAYERS = 2
EPS = 1e-6

A_HEADS = 8
A_DK = 128
A_DV = 128
CONV_W = 4
A_QK_DIM = A_HEADS * A_DK
A_V_DIM = A_HEADS * A_DV
A_CONV_DIM = 2 * A_QK_DIM + A_V_DIM
A_MAIN_DIM = A_CONV_DIM + A_V_DIM
GATE_PAD = 128

B_GROUPS = ((128, 1), (512, 4), (2048, 16))
N_GROUPS = 3
B_HEADS = 16
B_DH = 64
B_KV_HEADS = 4
B_QPK = B_HEADS // B_KV_HEADS
B_BLOCK = 128
B_KV_DIM = B_KV_HEADS * B_DH
B_Q_DIM = B_HEADS * B_DH
MAX_WINDOW = 2048

D_FF = 2816
FF_CHUNK = 256

LANES = 128
NEG = -1e30
V7X_VMEM_LIMIT = 56 * 1024 * 1024
HIGHEST = lax.Precision.HIGHEST


def _params(*sem):
    return pltpu.CompilerParams(dimension_semantics=sem, vmem_limit_bytes=V7X_VMEM_LIMIT)


def _resident(shape):
    nd = len(shape)
    return pl.BlockSpec(shape, lambda *_: (0,) * nd, pipeline_mode=pl.Buffered(1))


def _rms(x):
    return x * lax.rsqrt(jnp.mean(x * x, axis=-1, keepdims=True) + EPS)


def _silu(x):
    return x * jax.nn.sigmoid(x)


def _norm_proj_kernel(x_ref, g_ref, *refs, group_of, outs, n_chunk):
    nw = len(group_of)
    w_refs, o_refs = refs[:nw], refs[nw:]
    xs = _rms(x_ref[...])
    normed = {}
    for wi, grp in enumerate(group_of):
        if grp not in normed:
            normed[grp] = (xs * g_ref[grp:grp + 1, :]).astype(BF16)
        xn = normed[grp]
        n = w_refs[wi].shape[1]
        for n0 in range(0, n, n_chunk):
            n1 = min(n0 + n_chunk, n)
            res = jnp.dot(xn, w_refs[wi][:, n0:n1], preferred_element_type=F32)
            for oi, (src, _) in enumerate(outs):
                if src == wi:
                    o_refs[oi][:, n0:n1] = res.astype(o_refs[oi].dtype)


def _norm_proj(x, gains, weights, group_of, outs, tm):
    m, d = x.shape
    tm = min(tm, m)
    kern = functools.partial(_norm_proj_kernel, group_of=tuple(group_of), outs=tuple(outs), n_chunk=512)
    return pl.pallas_call(
        kern,
        grid=(m // tm,),
        in_specs=[pl.BlockSpec((tm, d), lambda i: (i, 0)), _resident(gains.shape)]
        + [_resident(w.shape) for w in weights],
        out_specs=[pl.BlockSpec((tm, weights[src].shape[1]), lambda i: (i, 0)) for src, _ in outs],
        out_shape=[jax.ShapeDtypeStruct((m, weights[src].shape[1]), dt) for src, dt in outs],
        compiler_params=_params("parallel"),
        name="norm_proj",
    )(x, gains, *weights)


def _gdn_kernel(pm_ref, bd_ref, hist_ref, s0_ref, cw_ref, ab_ref, og_ref,
                o_ref, s_ref, hnew_ref, xh_ref, *, bt, chunk, mm_dtype):
    nh, c = A_HEADS, chunk
    bh = bt * nh

    @pl.when(pl.program_id(1) == 0)
    def _start():
        s_ref[...] = s0_ref[...]
        xh_ref[:, 5:8, :] = hist_ref[...]

    x = pm_ref[:, :, 0:A_CONV_DIM]
    xh_ref[:, 8:8 + c, :] = x
    cw = cw_ref[...]
    conv = x * cw[CONV_W - 1:CONV_W, :]
    for j in range(1, CONV_W):
        conv = conv + xh_ref[:, 8 - j:8 - j + c, :] * cw[CONV_W - 1 - j:CONV_W - j, :]
    tail = xh_ref[:, 5 + c:8 + c, :]
    xh_ref[:, 5:8, :] = tail
    hnew_ref[...] = tail
    act = _silu(conv)

    def heads(base):
        return jnp.stack([act[bi, :, base + h * A_DK: base + (h + 1) * A_DK]
                          for bi in range(bt) for h in range(nh)])

    qr, kr, v = heads(0), heads(A_QK_DIM), heads(2 * A_QK_DIM)
    q = qr * (lax.rsqrt(jnp.sum(qr * qr, axis=-1, keepdims=True) + EPS) * (A_DK ** -0.5))
    k = kr * lax.rsqrt(jnp.sum(kr * kr, axis=-1, keepdims=True) + EPS)

    bd = bd_ref[...]
    beta_all = jax.nn.sigmoid(bd)
    z = bd + ab_ref[1:2, :]
    softplus = jnp.maximum(z, 0.0) + jnp.log1p(jnp.exp(-jnp.abs(z)))
    g_all = -jnp.exp(ab_ref[0:1, :]) * softplus

    ti = lax.broadcasted_iota(jnp.int32, (c, c), 0)
    si = lax.broadcasted_iota(jnp.int32, (c, c), 1)
    incl = ti >= si
    strict = ti > si
    tri = incl.astype(F32)
    gcol, grow, bcol, glast = [], [], [], []
    for bi in range(bt):
        gc = jnp.dot(tri, g_all[bi], precision=HIGHEST, preferred_element_type=F32)
        gct = gc.T
        for h in range(nh):
            gcol.append(gc[:, nh + h:nh + h + 1])
            grow.append(gct[nh + h:nh + h + 1, :])
            glast.append(gc[c - 1:c, nh + h:nh + h + 1])
            bcol.append(beta_all[bi][:, h:h + 1])
    gcol, grow, bcol, glast = jnp.stack(gcol), jnp.stack(grow), jnp.stack(bcol), jnp.stack(glast)

    decay = jnp.exp(jnp.where(incl, gcol - grow, NEG))
    gam = jnp.exp(gcol)

    def bmm(a, b):
        return jnp.einsum("bij,bjk->bik", a.astype(mm_dtype), b.astype(mm_dtype),
                          preferred_element_type=F32)

    def bmm_nt(a, b):
        return jnp.einsum("bid,bjd->bij", a.astype(mm_dtype), b.astype(mm_dtype),
                          preferred_element_type=F32)

    kk = bmm_nt(k, k)
    qk = bmm_nt(q, k)
    def bmm_split(a, b):
        a_hi = a.astype(BF16)
        a_lo = (a - a_hi.astype(F32)).astype(BF16)
        b_hi = b.astype(BF16)
        b_lo = (b - b_hi.astype(F32)).astype(BF16)
        return bmm(a_hi, b_hi) + bmm(a_hi, b_lo) + bmm(a_lo, b_hi)

    n_mat = jnp.where(strict, -(bcol * decay * kk), 0.0)
    m_pow = q_inv = n_mat
    for _ in range(int(math.log2(c)) - 1):
        m_pow = bmm(m_pow, m_pow)
        q_inv = q_inv + m_pow + bmm(q_inv, m_pow)
    if mm_dtype == BF16:
        resid = n_mat - q_inv + bmm_split(n_mat, q_inv)
        q_inv = q_inv + resid + bmm(q_inv, resid)
    rhs = jnp.concatenate([(bcol * gam) * k, bcol * v], axis=-1)
    sol = rhs + bmm(q_inv, rhs)
    w_mat, u_base = sol[..., :A_DK], sol[..., A_DK:]

    s_old = s_ref[...].reshape(bh, A_DK, A_DV)
    u = u_base - bmm(w_mat, s_old)
    o = bmm(gam * q, s_old) + bmm(decay * qk, u)
    k_dec = jnp.exp(glast - gcol) * k
    s_new = jnp.exp(glast) * s_old + bmm(jnp.swapaxes(k_dec, 1, 2), u)
    s_ref[...] = s_new.reshape(bt, nh, A_DK, A_DV)

    on = _rms(o) * og_ref[...]
    for bi in range(bt):
        for h in range(nh):
            gate = pm_ref[bi, :, A_CONV_DIM + h * A_DV:A_CONV_DIM + (h + 1) * A_DV]
            o_ref[bi, :, h * A_DV:(h + 1) * A_DV] = (on[bi * nh + h] * _silu(gate)).astype(o_ref.dtype)


def _gdn(pm, bd, hist, s0, layer, conv_w, ab, o_gain, bt, chunk, mm_dtype):
    b, t, _ = pm.shape
    bt = math.gcd(bt, b)
    kern = functools.partial(_gdn_kernel, bt=bt, chunk=chunk, mm_dtype=mm_dtype)
    return pl.pallas_call(
        kern,
        grid=(b // bt, t // chunk),
        in_specs=[
            pl.BlockSpec((bt, chunk, A_MAIN_DIM), lambda i, j: (i, j, 0)),
            pl.BlockSpec((bt, chunk, GATE_PAD), lambda i, j: (i, j, 0)),
            pl.BlockSpec((None, bt, CONV_W - 1, A_CONV_DIM), lambda i, j: (layer, i, 0, 0)),
            pl.BlockSpec((None, bt, A_HEADS, A_DK, A_DV), lambda i, j: (layer, i, 0, 0, 0)),
            pl.BlockSpec((CONV_W, A_CONV_DIM), lambda i, j: (0, 0)),
            pl.BlockSpec((2, GATE_PAD), lambda i, j: (0, 0)),
            pl.BlockSpec((1, A_DV), lambda i, j: (0, 0)),
        ],
        out_specs=[
            pl.BlockSpec((bt, chunk, A_V_DIM), lambda i, j: (i, j, 0)),
            pl.BlockSpec((bt, A_HEADS, A_DK, A_DV), lambda i, j: (i, 0, 0, 0)),
            pl.BlockSpec((bt, CONV_W - 1, A_CONV_DIM), lambda i, j: (i, 0, 0)),
        ],
        out_shape=[
            jax.ShapeDtypeStruct((b, t, A_V_DIM), BF16),
            jax.ShapeDtypeStruct((b, A_HEADS, A_DK, A_DV), F32),
            jax.ShapeDtypeStruct((b, CONV_W - 1, A_CONV_DIM), F32),
        ],
        scratch_shapes=[pltpu.VMEM((bt, chunk + 8, A_CONV_DIM), F32)],
        compiler_params=_params("parallel", "arbitrary"),
        name="gdn_mixer",
    )(pm, bd, hist, s0, conv_w, ab, o_gain)


def _mix_groups(a_refs, st_ref, ex_ref, nat_scr, tm):
    st = st_ref[...]
    ms = [st[:, g * 2 * B_HEADS:g * 2 * B_HEADS + B_HEADS] for g in range(N_GROUPS)]
    ls = [st[:, g * 2 * B_HEADS + B_HEADS:(g + 1) * 2 * B_HEADS] for g in range(N_GROUPS)]
    top = jnp.maximum(jnp.maximum(ms[0], ms[1]), ms[2])
    es = [jnp.exp(m - top) for m in ms]
    inv = 1.0 / (es[0] * ls[0] + es[1] * ls[1] + es[2] * ls[2])
    ex = ex_ref[...]
    nq = B_Q_DIM // LANES
    acc = None
    for (_, dil), e, a_ref in zip(B_GROUPS, es, a_refs):
        w = e * inv
        w_hi = w.astype(BF16)
        w_lo = (w - w_hi.astype(F32)).astype(BF16)
        wide = (jnp.dot(w_hi, ex, preferred_element_type=F32)
                + jnp.dot(w_lo, ex, preferred_element_type=F32))
        if dil == 1:
            part = a_ref[0, 0].astype(F32)
        else:
            rows = tm // dil
            for r in range(dil):
                val = a_ref[0, r].astype(F32)
                for c in range(nq):
                    nat_scr[c, pl.ds(r, rows, stride=dil), :] = val[:, c * LANES:(c + 1) * LANES]
            part = jnp.concatenate([nat_scr[c] for c in range(nq)], axis=1)
        term = wide * part
        acc = term if acc is None else acc + term
    return acc


def _post_ffn_kernel(x_ref, *refs, mixed, tm):
    if mixed:
        a_refs, (st_ref, ex_ref, wo_ref, g_ref, wi_ref, w2_ref, out_ref, acc_ref, nat_scr) = refs[:3], refs[3:]
        y_in = _mix_groups(a_refs, st_ref, ex_ref, nat_scr, tm).astype(BF16)
    else:
        y_ref, wo_ref, g_ref, wi_ref, w2_ref, out_ref, acc_ref = refs
        y_in = y_ref[...]
    y = jnp.dot(y_in, wo_ref[...], preferred_element_type=F32)
    x1 = x_ref[...] + _rms(y) * g_ref[0:1, :]
    out_ref[...] = x1
    hn = (_rms(x1) * g_ref[1:2, :]).astype(BF16)
    for j in range(0, D_FF, FF_CHUNK):
        gate = jnp.dot(hn, wi_ref[:, j:j + FF_CHUNK], preferred_element_type=F32)
        up = jnp.dot(hn, wi_ref[:, D_FF + j:D_FF + j + FF_CHUNK], preferred_element_type=F32)
        part = jnp.dot((_silu(gate) * up).astype(BF16), w2_ref[j:j + FF_CHUNK, :],
                       preferred_element_type=F32)
        if j == 0:
            acc_ref[...] = part
        else:
            acc_ref[...] += part
    out_ref[...] += _rms(acc_ref[...]) * g_ref[2:3, :]


def _post_ffn(x, y, w_o, gains, w_in, w_out, tm, seq=None):
    m, d = x.shape
    tm = min(tm, m)
    mixed = seq is not None
    row = lambda n: pl.BlockSpec((tm, n), lambda i: (i, 0))
    scratch = [pltpu.VMEM((tm, d), F32)]
    if mixed:
        accs, stats = y
        tiles = seq // tm
        ex = jnp.asarray(np.repeat(np.eye(B_HEADS, dtype=np.float32), B_DH, axis=1), BF16)
        y_args = [*accs, stats, ex]
        y_specs = [pl.BlockSpec((1, dil, tm // dil, B_Q_DIM), lambda i: (i // tiles, 0, i % tiles, 0))
                   for _, dil in B_GROUPS] + [row(stats.shape[1]), _resident(ex.shape)]
        scratch.append(pltpu.VMEM((B_Q_DIM // LANES, tm, LANES), F32))
    else:
        y_args, y_specs = [y], [row(y.shape[1])]
    return pl.pallas_call(
        functools.partial(_post_ffn_kernel, mixed=mixed, tm=tm),
        grid=(m // tm,),
        in_specs=[row(d)] + y_specs
        + [_resident(w_o.shape), _resident(gains.shape), _resident(w_in.shape), _resident(w_out.shape)],
        out_specs=row(d),
        out_shape=jax.ShapeDtypeStruct((m, d), F32),
        scratch_shapes=scratch,
        compiler_params=_params("parallel"),
        name="post_ffn",
    )(x, *y_args, w_o, gains, w_in, w_out)


def _decimate_rows(src_ref, n_chunks, dil, rows):
    return jnp.concatenate(
        [jnp.concatenate([src_ref[c, pl.ds(r, rows, stride=dil), :] for c in range(n_chunks)], axis=1)
         for r in range(dil)], axis=0)


def _b_proj_kernel(x_ref, g_ref, *refs, with_kv, ts):
    if with_kv:
        wk_ref, wv_ref, wq_ref, k_ref, v_ref = refs[:5]
        kd_refs, vd_refs, qd_refs = refs[5:8], refs[8:11], refs[11:14]
        xn_scr, kv_scr = refs[14:]
    else:
        wq_ref = refs[0]
        qd_refs = refs[1:4]
        (xn_scr,) = refs[4:]
    xs = _rms(x_ref[...])
    q_gain = 1 if with_kv else 0
    xq = xs * g_ref[q_gain:q_gain + 1, :]
    nq = D_MODEL // LANES
    for c in range(nq):
        xn_scr[c] = xq[:, c * LANES:(c + 1) * LANES]
    for gi, (_, dil) in enumerate(B_GROUPS):
        rows = ts // dil
        lhs = (xq if dil == 1 else _decimate_rows(xn_scr, nq, dil, rows)).astype(BF16)
        for n0 in range(0, B_Q_DIM, 512):
            res = jnp.dot(lhs, wq_ref[:, gi * B_Q_DIM + n0:gi * B_Q_DIM + n0 + 512], preferred_element_type=F32)
            qd_refs[gi][0, :, :, n0:n0 + 512] = res.reshape(dil, rows, 512).astype(BF16)
    if with_kv:
        xkv = (xs * g_ref[0:1, :]).astype(BF16)
        nk = B_KV_DIM // LANES
        for w_ref, nat_ref, d_refs in ((wk_ref, k_ref, kd_refs), (wv_ref, v_ref, vd_refs)):
            val = jnp.dot(xkv, w_ref[...], preferred_element_type=F32)
            nat_ref[...] = val
            for c in range(nk):
                kv_scr[c] = val[:, c * LANES:(c + 1) * LANES]
            for gi, (_, dil) in enumerate(B_GROUPS):
                rows = ts // dil
                dec = val if dil == 1 else _decimate_rows(kv_scr, nk, dil, rows)
                d_refs[gi][0] = dec.reshape(dil, rows, B_KV_DIM).astype(BF16)


def _b_proj(x, gains, weights, b, s, with_kv, ts):
    m, d = x.shape
    tiles = s // ts
    dec = lambda n: [pl.BlockSpec((1, dil, ts // dil, n), lambda i: (i // tiles, 0, i % tiles, 0))
                     for _, dil in B_GROUPS]
    dec_shape = lambda n: [jax.ShapeDtypeStruct((b, dil, s // dil, n), BF16) for _, dil in B_GROUPS]
    nat = pl.BlockSpec((ts, B_KV_DIM), lambda i: (i, 0))
    out_specs, out_shape = dec(B_Q_DIM), dec_shape(B_Q_DIM)
    scratch = [pltpu.VMEM((D_MODEL // LANES, ts, LANES), F32)]
    if with_kv:
        out_specs = [nat, nat] + dec(B_KV_DIM) + dec(B_KV_DIM) + out_specs
        out_shape = [jax.ShapeDtypeStruct((m, B_KV_DIM), F32)] * 2 + dec_shape(B_KV_DIM) * 2 + out_shape
        scratch.append(pltpu.VMEM((B_KV_DIM // LANES, ts, LANES), F32))
    return pl.pallas_call(
        functools.partial(_b_proj_kernel, with_kv=with_kv, ts=ts),
        grid=(m // ts,),
        in_specs=[pl.BlockSpec((ts, d), lambda i: (i, 0)), _resident(gains.shape)]
        + [_resident(w.shape) for w in weights],
        out_specs=out_specs,
        out_shape=out_shape,
        scratch_shapes=scratch,
        compiler_params=_params("parallel"),
        name="attn_proj",
    )(x, gains, *weights)


def _alibi_slopes():
    n = N_GROUPS * B_HEADS
    return (2.0 ** (-8.0 * np.arange(1, n + 1) / n)).astype(np.float32).reshape(N_GROUPS, B_HEADS)


def _band_bias(gi):
    _, dil = B_GROUPS[gi]
    qi = np.arange(B_BLOCK)[None, :]
    ci = np.arange(B_BLOCK)[:, None]
    delta = np.where(ci > qi, B_BLOCK + qi - ci, qi - ci).astype(np.float32)
    rest = -_alibi_slopes()[gi][:, None, None] * (dil * delta)[None]
    first = np.where((ci > qi)[None], np.float32(NEG), rest)
    return np.stack([first, rest]).astype(np.float32)


def _band_attn_kernel(q_ref, kp_ref, kc_ref, vp_ref, vc_ref, bias_ref, o_ref, st_ref,
                      sp_scr, sc_scr, eu_scr, el_scr, *, far_bias, bb):
    blk = pl.program_id(2)
    first = blk == 0
    sel = jnp.minimum(blk, 1)
    ci = lax.broadcasted_iota(jnp.int32, (B_BLOCK, B_BLOCK), 0)
    qi = lax.broadcasted_iota(jnp.int32, (B_BLOCK, B_BLOCK), 1)
    upper = ci > qi
    diag = ci == qi
    nt = (((1,), (1,)), ((), ()))
    for bi in range(bb):
        for g in range(B_KV_HEADS):
            cols = slice(g * B_DH, (g + 1) * B_DH)
            slot = bi * B_KV_HEADS + g
            qg = jnp.concatenate([q_ref[bi, 0, :, (g * B_QPK + p) * B_DH:(g * B_QPK + p + 1) * B_DH]
                                  for p in range(B_QPK)], axis=0)
            sp_scr[slot] = lax.dot_general(kp_ref[bi, 0, :, cols], qg, nt, preferred_element_type=F32)
            sc_scr[slot] = lax.dot_general(kc_ref[bi, 0, :, cols], qg, nt, preferred_element_type=F32)
    for bi in range(bb):
        for h in range(B_HEADS):
            g, p = divmod(h, B_QPK)
            slot = bi * B_KV_HEADS + g
            lanes = slice(p * B_BLOCK, (p + 1) * B_BLOCK)
            sp = sp_scr[slot, :, lanes]
            s = jnp.where(upper, sp, sc_scr[slot, :, lanes]) + bias_ref[sel, h]
            far = (jnp.sum(jnp.where(diag, sp, 0.0), axis=0, keepdims=True)
                   + jnp.where(first, NEG, far_bias[h]))
            mx = jnp.maximum(jnp.max(s, axis=0, keepdims=True), far)
            e = jnp.exp(s - mx)
            e_far = jnp.exp(far - mx)
            st_ref[bi, 0, 0, h:h + 1, :] = mx
            st_ref[bi, 0, 0, B_HEADS + h:B_HEADS + h + 1, :] = jnp.sum(e, axis=0, keepdims=True) + e_far
            eu_scr[slot, :, lanes] = jnp.where(upper, e, jnp.where(diag, e_far, 0.0)).astype(BF16)
            el_scr[slot, :, lanes] = jnp.where(upper, 0.0, e).astype(BF16)
    for bi in range(bb):
        for g in range(B_KV_HEADS):
            cols = slice(g * B_DH, (g + 1) * B_DH)
            slot = bi * B_KV_HEADS + g
            pv_t = (jnp.dot(vp_ref[bi, 0, :, cols].T, eu_scr[slot], preferred_element_type=F32)
                    + jnp.dot(vc_ref[bi, 0, :, cols].T, el_scr[slot], preferred_element_type=F32))
            pv = pv_t.T
            for p in range(B_QPK):
                h = g * B_QPK + p
                o_ref[bi, 0, :, h * B_DH:(h + 1) * B_DH] = pv[p * B_BLOCK:(p + 1) * B_BLOCK].astype(o_ref.dtype)


def _band_attn(qd, kd, vd, gi, bb):
    win, dil = B_GROUPS[gi]
    assert win // dil == B_BLOCK
    b, _, n_dec, _ = qd.shape
    nb = n_dec // B_BLOCK
    bias = jnp.asarray(_band_bias(gi))
    far_bias = tuple(float(-sl * win) for sl in _alibi_slopes()[gi])
    cur = lambda bi, r, j: (bi, r, j, 0)
    prev = lambda bi, r, j: (bi, r, jnp.maximum(j - 1, 0), 0)
    bb = math.gcd(bb, b)
    blk = lambda n, imap: pl.BlockSpec((bb, 1, B_BLOCK, n), imap)
    tiles = (bb * B_KV_HEADS, B_BLOCK, B_QPK * B_BLOCK)
    return pl.pallas_call(
        functools.partial(_band_attn_kernel, far_bias=far_bias, bb=bb),
        grid=(b // bb, dil, nb),
        in_specs=[blk(B_Q_DIM, cur), blk(B_KV_DIM, prev), blk(B_KV_DIM, cur), blk(B_KV_DIM, prev),
                  blk(B_KV_DIM, cur), _resident(bias.shape)],
        out_specs=[blk(B_Q_DIM, cur),
                   pl.BlockSpec((bb, 1, 1, 2 * B_HEADS, B_BLOCK), lambda bi, r, j: (bi, r, j, 0, 0))],
        out_shape=[jax.ShapeDtypeStruct((b, dil, n_dec, B_Q_DIM), BF16),
                   jax.ShapeDtypeStruct((b, dil, nb, 2 * B_HEADS, B_BLOCK), F32)],
        scratch_shapes=[pltpu.VMEM(tiles, F32), pltpu.VMEM(tiles, F32),
                        pltpu.VMEM(tiles, BF16), pltpu.VMEM(tiles, BF16)],
        compiler_params=_params("parallel", "parallel", "arbitrary"),
        name=f"band_attn_g{gi}",
    )(qd, kd, kd, vd, vd, bias)


def _decode_rows(t_past, l_new):
    _, dil_far = B_GROUPS[-1]
    assert t_past % dil_far == 0 and l_new <= dil_far
    tail_start = max(0, t_past - max(win for win, _ in B_GROUPS[:-1])) // dil_far * dil_far
    far = np.arange(tail_start).reshape(-1, dil_far)[:, :l_new].reshape(-1)
    return tail_start, np.concatenate([far, np.arange(tail_start, t_past)])


def _decode_ranges(t_past, l_new):
    _, pos = _decode_rows(t_past, l_new)
    return tuple(int(np.searchsorted(pos, max(0, t_past - win))) // 16 * 16 for win, _ in B_GROUPS)


def _decode_bias(l_new, t_past):
    slopes = _alibi_slopes()
    rows_grp = B_KV_HEADS * B_QPK * l_new
    starts = _decode_ranges(t_past, l_new)
    _, kept = _decode_rows(t_past, l_new)
    bias_c = [np.full((rows_grp, len(kept) - st), NEG, np.float32) for st in starts]
    bias_n = np.full((N_GROUPS * rows_grp, l_new), NEG, np.float32)
    for gi, (win, dil) in enumerate(B_GROUPS):
        r = 0
        for g in range(B_KV_HEADS):
            for p in range(B_QPK):
                for l in range(l_new):
                    for pos, tbl, row in ((kept[starts[gi]:], bias_c[gi], r),
                                          (t_past + np.arange(l_new), bias_n, gi * rows_grp + r)):
                        dist = t_past + l - pos
                        ok = (dist >= 0) & (dist <= win) & (dist % dil == 0)
                        tbl[row] = np.where(ok, -slopes[gi, g * B_QPK + p] * dist.astype(np.float32), NEG)
                    r += 1
    return bias_c, bias_n


def _decode_attn_kernel(q_ref, kf_ref, kt_ref, vf_ref, vt_ref, kn_ref, vn_ref, b0_ref, b1_ref, b2_ref, bn_ref,
                        o_ref, *, l_new, bb, starts):
    rows_g = B_QPK * l_new
    rows_grp = B_KV_HEADS * rows_g
    lane = lax.broadcasted_iota(jnp.int32, (rows_g, B_KV_DIM), 1)
    keep = [(lane >= g * B_DH) & (lane < (g + 1) * B_DH) for g in range(B_KV_HEADS)]
    nt = (((1,), (1,)), ((), ()))
    n_far = kf_ref.shape[1] * kf_ref.shape[2]
    n_tail = kt_ref.shape[1] * kt_ref.shape[2]
    for bi in range(bb):
        qt = q_ref[bi]
        k_far, v_far = kf_ref[bi].reshape(n_far, B_KV_DIM), vf_ref[bi].reshape(n_far, B_KV_DIM)
        k_tail, v_tail = kt_ref[bi].reshape(n_tail, B_KV_DIM), vt_ref[bi].reshape(n_tail, B_KV_DIM)
        ms, dens, accs = [], [], []
        for gi, b_ref in enumerate((b0_ref, b1_ref, b2_ref)):
            qg = qt[gi * rows_g:(gi + 1) * rows_g]
            qbd = jnp.concatenate([jnp.where(keep[g], qg, 0.0) for g in range(B_KV_HEADS)], axis=0)
            st = starts[gi]
            if st < n_far:
                parts = [(k_far[st:], v_far[st:], b_ref[:, :n_far - st]), (k_tail, v_tail, b_ref[:, n_far - st:])]
            else:
                parts = [(k_tail[st - n_far:], v_tail[st - n_far:], b_ref[...])]
            parts.append((kn_ref[bi], vn_ref[bi], bn_ref[gi * rows_grp:(gi + 1) * rows_grp, :]))
            scores = [lax.dot_general(qbd, k, nt, preferred_element_type=F32) + bias for k, _, bias in parts]
            mx = functools.reduce(jnp.maximum, [jnp.max(s, axis=-1, keepdims=True) for s in scores])
            es = [jnp.exp(s - mx) for s in scores]
            ms.append(mx)
            dens.append(sum(jnp.sum(e, axis=-1, keepdims=True) for e in es))
            accs.append(sum(jnp.dot(e, v, preferred_element_type=F32)
                            for e, (_, v, _) in zip(es, parts)))
        top = jnp.maximum(jnp.maximum(ms[0], ms[1]), ms[2])
        num, tot = None, None
        for i in range(N_GROUPS):
            a = jnp.exp(ms[i] - top)
            num = a * accs[i] if num is None else num + a * accs[i]
            tot = a * dens[i] if tot is None else tot + a * dens[i]
        out = num / tot
        o_ref[bi] = jnp.concatenate([out[g * rows_g:(g + 1) * rows_g, g * B_DH:(g + 1) * B_DH]
                                     for g in range(B_KV_HEADS)], axis=0)


def _decode_attn(q, k_new, v_new, cache_k, cache_v, b, l_new, bb):
    _, dil_far = B_GROUPS[-1]
    t_past = cache_k.shape[1] * dil_far
    tail_start, _ = _decode_rows(t_past, l_new)
    far_blocks, tail_blocks = tail_start // dil_far, (t_past - tail_start) // dil_far
    assert far_blocks % tail_blocks == 0 and l_new % 8 == 0
    rows_g = B_QPK * l_new
    qt = q.reshape(b, l_new, N_GROUPS, B_KV_HEADS, B_QPK, B_DH)
    qt = jnp.transpose(qt, (0, 2, 4, 1, 3, 5)).reshape(b, N_GROUPS * rows_g, B_KV_DIM)
    bias_c, bias_n = _decode_bias(l_new, t_past)
    tables = [jnp.asarray(a) for a in (*bias_c, bias_n)]
    seq = lambda n, d: pl.BlockSpec((bb, n, d), lambda i: (i, 0, 0))
    far = pl.BlockSpec((bb, far_blocks, l_new, B_KV_DIM), lambda i: (i, 0, 0, 0))
    tail = pl.BlockSpec((bb, tail_blocks, dil_far, B_KV_DIM), lambda i: (i, far_blocks // tail_blocks, 0, 0))
    out = pl.pallas_call(
        functools.partial(_decode_attn_kernel, l_new=l_new, bb=bb, starts=_decode_ranges(t_past, l_new)),
        grid=(b // bb,),
        in_specs=[seq(N_GROUPS * rows_g, B_KV_DIM), far, tail, far, tail,
                  seq(l_new, B_KV_DIM), seq(l_new, B_KV_DIM)] + [_resident(a.shape) for a in tables],
        out_specs=seq(B_KV_HEADS * rows_g, B_DH),
        out_shape=jax.ShapeDtypeStruct((b, B_KV_HEADS * rows_g, B_DH), F32),
        compiler_params=_params("parallel"),
        name="decode_attn",
    )(qt, cache_k, cache_k, cache_v, cache_v, k_new.reshape(b, l_new, B_KV_DIM),
      v_new.reshape(b, l_new, B_KV_DIM), *tables)
    out = out.reshape(b, B_HEADS, l_new, B_DH)
    return jnp.transpose(out, (0, 2, 1, 3)).reshape(b * l_new, B_Q_DIM).astype(BF16)


def _stats_token_order(st):
    b, dil, nb, n, blk = st.shape
    return jnp.transpose(st, (0, 2, 4, 1, 3)).reshape(b * dil * nb * blk, n)


def _trunk(x3, conv_hist, delta_s, kv_past, w, *, tm_proj, tm_ffn, gdn_bt, gdn_chunk, gdn_mm):
    b, t, d = x3.shape
    m = b * t
    x = x3.reshape(m, d)
    new_hist, new_delta = [], []
    k_new = v_new = kd = vd = None
    for layer in range(DEPTH):
        if layer < N_A_LAYERS:
            pm, bd = _norm_proj(x, w["norms"][layer, 0:1], [w["a_w_main"][layer], w["a_w_gate"][layer]],
                                (0, 0), ((0, F32), (1, F32)), tm_proj)
            y, s_new, h_new = _gdn(pm.reshape(b, t, A_MAIN_DIM), bd.reshape(b, t, GATE_PAD),
                                   conv_hist, delta_s, layer, w["a_conv_w"][layer], w["a_ab"][layer],
                                   w["a_o_gain"][layer], gdn_bt, gdn_chunk, gdn_mm)
            y = y.reshape(m, A_V_DIM)
            new_hist.append(h_new)
            new_delta.append(s_new)
            w_mix = w["a_w_out"][layer]
        else:
            j = layer - N_A_LAYERS
            q_gain = w["norms"][layer, 0]
            if kv_past is None:
                if j == 0:
                    res = _b_proj(x, jnp.stack([w["kv_norm"], q_gain]), [w["b_w_k"], w["b_w_v"], w["b_w_q"][j]],
                                  b, t, True, tm_ffn)
                    k_new, v_new, kd, vd, qd = res[0], res[1], res[2:5], res[5:8], res[8:11]
                else:
                    qd = _b_proj(x, q_gain[None], [w["b_w_q"][j]], b, t, False, tm_ffn)
                parts = [_band_attn(qd[gi], kd[gi], vd[gi], gi, 2) for gi in range(N_GROUPS)]
                stats = jnp.concatenate([_stats_token_order(p[1]) for p in parts], axis=1)
                y = ([p[0] for p in parts], stats)
            else:
                if j == 0:
                    k_new, v_new, q = _norm_proj(
                        x, jnp.stack([w["kv_norm"], q_gain]), [w["b_w_k"], w["b_w_v"], w["b_w_q"][j]],
                        (0, 0, 1), ((0, F32), (1, F32), (2, F32)), tm_proj)
                else:
                    (q,) = _norm_proj(x, q_gain[None], [w["b_w_q"][j]], (0,), ((0, F32),), tm_proj)
                y = _decode_attn(q, k_new, v_new, *kv_past, b, t, 2)
            w_mix = w["b_w_o"][j]
        x = _post_ffn(x, y, w_mix, w["norms"][layer, 1:4], w["ffn_w_in"][layer], w["ffn_w_out"][layer], tm_ffn,
                      seq=t if isinstance(y, tuple) else None)
    return (x.reshape(b, t, d), jnp.stack(new_hist), jnp.stack(new_delta),
            k_new.reshape(b, t, B_KV_HEADS, B_DH), v_new.reshape(b, t, B_KV_HEADS, B_DH))


def kernel(x_prompt, x_sample, state_conv, state_delta, cache_k, cache_v, norms, kv_norm, a_w_in,
           a_conv_w, a_log, a_dt_bias, a_o_gain, a_w_out, b_w_kv, b_w_q, b_w_o, ffn_w_in, ffn_w_out):
    bp, sp, _ = x_prompt.shape
    gate_w = a_w_in[:, :, A_MAIN_DIM:]
    gate_w = jnp.pad(gate_w, ((0, 0), (0, 0), (0, GATE_PAD - gate_w.shape[-1])))
    ab = jnp.pad(jnp.stack([a_log, a_dt_bias], axis=1), ((0, 0), (0, 0), (A_HEADS, GATE_PAD - 2 * A_HEADS)))
    w_kv = b_w_kv.reshape(D_MODEL, 2, B_KV_DIM)
    w = {
        "norms": norms, "kv_norm": kv_norm,
        "a_w_main": a_w_in[:, :, :A_MAIN_DIM].astype(BF16), "a_w_gate": gate_w.astype(BF16),
        "a_conv_w": a_conv_w, "a_ab": ab, "a_o_gain": a_o_gain.reshape(N_A_LAYERS, 1, A_DV),
        "a_w_out": a_w_out.astype(BF16),
        "b_w_k": w_kv[:, 0].astype(BF16), "b_w_v": w_kv[:, 1].astype(BF16),
        "b_w_q": (b_w_q * (B_DH ** -0.5)).astype(BF16), "b_w_o": b_w_o.astype(BF16),
        "ffn_w_in": ffn_w_in.astype(BF16), "ffn_w_out": ffn_w_out.astype(BF16),
    }
    zero_hist = jnp.zeros((N_A_LAYERS, bp, CONV_W - 1, A_CONV_DIM), x_prompt.dtype)
    zero_delta = jnp.zeros((N_A_LAYERS, bp, A_HEADS, A_DK, A_DV), x_prompt.dtype)
    y_p, conv_p, delta_p, k_p, v_p = _trunk(
        x_prompt, zero_hist, zero_delta, None, w,
        tm_proj=256, tm_ffn=512, gdn_bt=2, gdn_chunk=128, gdn_mm=BF16)
    bs, t_past = cache_k.shape[:2]
    dil_far = B_GROUPS[-1][1]
    kv_past = tuple(c.reshape(bs, t_past // dil_far, dil_far, B_KV_DIM) for c in (cache_k, cache_v))
    y_s, conv_s, delta_s, k_s, v_s = _trunk(
        x_sample, state_conv, state_delta, kv_past, w,
        tm_proj=256, tm_ffn=512, gdn_bt=8, gdn_chunk=x_sample.shape[1], gdn_mm=F32)
    win_p = min(MAX_WINDOW, sp)
    return (y_p, y_s, conv_p, delta_p, k_p[:, sp - win_p:], v_p[:, sp - win_p:], conv_s, delta_s, k_s, v_s)
```

```python
import functools
import math

import jax
import jax.numpy as jnp
import numpy as np
from jax import lax
from jax.experimental import pallas as pl
from jax.experimental.pallas import tpu as pltpu

F32 = jnp.float32
BF16 = jnp.bfloat16

D_MODEL = 1024
DEPTH = 4
N_A_LAYERS = 2
EPS = 1e-6

A_HEADS = 8
A_DK = 128
A_DV = 128
CONV_W = 4
A_QK_DIM = A_HEADS * A_DK
A_V_DIM = A_HEADS * A_DV
A_CONV_DIM = 2 * A_QK_DIM + A_V_DIM
A_MAIN_DIM = A_CONV_DIM + A_V_DIM
GATE_PAD = 128

B_GROUPS = ((128, 1), (512, 4), (2048, 16))
N_GROUPS = 3
B_HEADS = 16
B_DH = 64
B_KV_HEADS = 4
B_QPK = B_HEADS // B_KV_HEADS
B_BLOCK = 128
B_KV_DIM = B_KV_HEADS * B_DH
B_Q_DIM = B_HEADS * B_DH
MAX_WINDOW = 2048

D_FF = 2816
FF_CHUNK = 256

LANES = 128
NEG = -1e30
V7X_VMEM_LIMIT = 56 * 1024 * 1024
HIGHEST = lax.Precision.HIGHEST


def _params(*sem):
    return pltpu.CompilerParams(dimension_semantics=sem, vmem_limit_bytes=V7X_VMEM_LIMIT)


def _resident(shape):
    nd = len(shape)
    return pl.BlockSpec(shape, lambda *_: (0,) * nd, pipeline_mode=pl.Buffered(1))


def _rms(x):
    return x * lax.rsqrt(jnp.mean(x * x, axis=-1, keepdims=True) + EPS)


def _silu(x):
    return x * jax.nn.sigmoid(x)


def _norm_proj_kernel(x_ref, g_ref, *refs, group_of, outs, n_chunk):
    nw = len(group_of)
    w_refs, o_refs = refs[:nw], refs[nw:]
    xs = _rms(x_ref[...])
    normed = {}
    for wi, grp in enumerate(group_of):
        if grp not in normed:
            normed[grp] = (xs * g_ref[grp:grp + 1, :]).astype(BF16)
        xn = normed[grp]
        n = w_refs[wi].shape[1]
        for n0 in range(0, n, n_chunk):
            n1 = min(n0 + n_chunk, n)
            res = jnp.dot(xn, w_refs[wi][:, n0:n1], preferred_element_type=F32)
            for oi, (src, _) in enumerate(outs):
                if src == wi:
                    o_refs[oi][:, n0:n1] = res.astype(o_refs[oi].dtype)


def _norm_proj(x, gains, weights, group_of, outs, tm):
    m, d = x.shape
    tm = min(tm, m)
    kern = functools.partial(_norm_proj_kernel, group_of=tuple(group_of), outs=tuple(outs), n_chunk=512)
    return pl.pallas_call(
        kern,
        grid=(m // tm,),
        in_specs=[pl.BlockSpec((tm, d), lambda i: (i, 0)), _resident(gains.shape)]
        + [_resident(w.shape) for w in weights],
        out_specs=[pl.BlockSpec((tm, weights[src].shape[1]), lambda i: (i, 0)) for src, _ in outs],
        out_shape=[jax.ShapeDtypeStruct((m, weights[src].shape[1]), dt) for src, dt in outs],
        compiler_params=_params("parallel"),
        name="norm_proj",
    )(x, gains, *weights)


def _gdn_kernel(pm_ref, bd_ref, hist_ref, s0_ref, cw_ref, ab_ref, og_ref,
                o_ref, s_ref, hnew_ref, xh_ref, *, bt, chunk, mm_dtype):
    nh, c = A_HEADS, chunk
    bh = bt * nh

    @pl.when(pl.program_id(1) == 0)
    def _start():
        s_ref[...] = s0_ref[...]
        xh_ref[:, 5:8, :] = hist_ref[...]

    x = pm_ref[:, :, 0:A_CONV_DIM]
    xh_ref[:, 8:8 + c, :] = x
    cw = cw_ref[...]
    conv = x * cw[CONV_W - 1:CONV_W, :]
    for j in range(1, CONV_W):
        conv = conv + xh_ref[:, 8 - j:8 - j + c, :] * cw[CONV_W - 1 - j:CONV_W - j, :]
    tail = xh_ref[:, 5 + c:8 + c, :]
    xh_ref[:, 5:8, :] = tail
    hnew_ref[...] = tail
    act = _silu(conv)

    def heads(base):
        return jnp.stack([act[bi, :, base + h * A_DK: base + (h + 1) * A_DK]
                          for bi in range(bt) for h in range(nh)])

    qr, kr, v = heads(0), heads(A_QK_DIM), heads(2 * A_QK_DIM)
    q = qr * (lax.rsqrt(jnp.sum(qr * qr, axis=-1, keepdims=True) + EPS) * (A_DK ** -0.5))
    k = kr * lax.rsqrt(jnp.sum(kr * kr, axis=-1, keepdims=True) + EPS)

    bd = bd_ref[...]
    beta_all = jax.nn.sigmoid(bd)
    z = bd + ab_ref[1:2, :]
    softplus = jnp.maximum(z, 0.0) + jnp.log1p(jnp.exp(-jnp.abs(z)))
    g_all = -jnp.exp(ab_ref[0:1, :]) * softplus

    ti = lax.broadcasted_iota(jnp.int32, (c, c), 0)
    si = lax.broadcasted_iota(jnp.int32, (c, c), 1)
    incl = ti >= si
    strict = ti > si
    tri = incl.astype(F32)
    gcol, grow, bcol, glast = [], [], [], []
    for bi in range(bt):
        gc = jnp.dot(tri, g_all[bi], precision=HIGHEST, preferred_element_type=F32)
        gct = gc.T
        for h in range(nh):
            gcol.append(gc[:, nh + h:nh + h + 1])
            grow.append(gct[nh + h:nh + h + 1, :])
            glast.append(gc[c - 1:c, nh + h:nh + h + 1])
            bcol.append(beta_all[bi][:, h:h + 1])
    gcol, grow, bcol, glast = jnp.stack(gcol), jnp.stack(grow), jnp.stack(bcol), jnp.stack(glast)

    decay = jnp.exp(jnp.where(incl, gcol - grow, NEG))
    gam = jnp.exp(gcol)

    def bmm(a, b):
        return jnp.einsum("bij,bjk->bik", a.astype(mm_dtype), b.astype(mm_dtype),
                          preferred_element_type=F32)

    def bmm_nt(a, b):
        return jnp.einsum("bid,bjd->bij", a.astype(mm_dtype), b.astype(mm_dtype),
                          preferred_element_type=F32)

    kk = bmm_nt(k, k)
    qk = bmm_nt(q, k)
    def bmm_split(a, b):
        a_hi = a.astype(BF16)
        a_lo = (a - a_hi.astype(F32)).astype(BF16)
        b_hi = b.astype(BF16)
        b_lo = (b - b_hi.astype(F32)).astype(BF16)
        return bmm(a_hi, b_hi) + bmm(a_hi, b_lo) + bmm(a_lo, b_hi)

    n_mat = jnp.where(strict, -(bcol * decay * kk), 0.0)
    m_pow = q_inv = n_mat
    for _ in range(int(math.log2(c)) - 1):
        m_pow = bmm(m_pow, m_pow)
        q_inv = q_inv + m_pow + bmm(q_inv, m_pow)
    if mm_dtype == BF16:
        resid = n_mat - q_inv + bmm_split(n_mat, q_inv)
        q_inv = q_inv + resid + bmm(q_inv, resid)
    rhs = jnp.concatenate([(bcol * gam) * k, bcol * v], axis=-1)
    sol = rhs + bmm(q_inv, rhs)
    w_mat, u_base = sol[..., :A_DK], sol[..., A_DK:]

    s_old = s_ref[...].reshape(bh, A_DK, A_DV)
    u = u_base - bmm(w_mat, s_old)
    o = bmm(gam * q, s_old) + bmm(decay * qk, u)
    k_dec = jnp.exp(glast - gcol) * k
    s_new = jnp.exp(glast) * s_old + bmm(jnp.swapaxes(k_dec, 1, 2), u)
    s_ref[...] = s_new.reshape(bt, nh, A_DK, A_DV)

    on = _rms(o) * og_ref[...]
    for bi in range(bt):
        for h in range(nh):
            gate = pm_ref[bi, :, A_CONV_DIM + h * A_DV:A_CONV_DIM + (h + 1) * A_DV]
            o_ref[bi, :, h * A_DV:(h + 1) * A_DV] = (on[bi * nh + h] * _silu(gate)).astype(o_ref.dtype)


def _gdn(pm, bd, hist, s0, layer, conv_w, ab, o_gain, bt, chunk, mm_dtype):
    b, t, _ = pm.shape
    bt = math.gcd(bt, b)
    kern = functools.partial(_gdn_kernel, bt=bt, chunk=chunk, mm_dtype=mm_dtype)
    return pl.pallas_call(
        kern,
        grid=(b // bt, t // chunk),
        in_specs=[
            pl.BlockSpec((bt, chunk, A_MAIN_DIM), lambda i, j: (i, j, 0)),
            pl.BlockSpec((bt, chunk, GATE_PAD), lambda i, j: (i, j, 0)),
            pl.BlockSpec((None, bt, CONV_W - 1, A_CONV_DIM), lambda i, j: (layer, i, 0, 0)),
            pl.BlockSpec((None, bt, A_HEADS, A_DK, A_DV), lambda i, j: (layer, i, 0, 0, 0)),
            pl.BlockSpec((CONV_W, A_CONV_DIM), lambda i, j: (0, 0)),
            pl.BlockSpec((2, GATE_PAD), lambda i, j: (0, 0)),
            pl.BlockSpec((1, A_DV), lambda i, j: (0, 0)),
        ],
        out_specs=[
            pl.BlockSpec((bt, chunk, A_V_DIM), lambda i, j: (i, j, 0)),
            pl.BlockSpec((bt, A_HEADS, A_DK, A_DV), lambda i, j: (i, 0, 0, 0)),
            pl.BlockSpec((bt, CONV_W - 1, A_CONV_DIM), lambda i, j: (i, 0, 0)),
        ],
        out_shape=[
            jax.ShapeDtypeStruct((b, t, A_V_DIM), BF16),
            jax.ShapeDtypeStruct((b, A_HEADS, A_DK, A_DV), F32),
            jax.ShapeDtypeStruct((b, CONV_W - 1, A_CONV_DIM), F32),
        ],
        scratch_shapes=[pltpu.VMEM((bt, chunk + 8, A_CONV_DIM), F32)],
        compiler_params=_params("parallel", "arbitrary"),
        name="gdn_mixer",
    )(pm, bd, hist, s0, conv_w, ab, o_gain)


def _mix_groups(a_refs, st_ref, ex_ref, nat_scr, tm):
    st = st_ref[...]
    ms = [st[:, g * 2 * B_HEADS:g * 2 * B_HEADS + B_HEADS] for g in range(N_GROUPS)]
    ls = [st[:, g * 2 * B_HEADS + B_HEADS:(g + 1) * 2 * B_HEADS] for g in range(N_GROUPS)]
    top = jnp.maximum(jnp.maximum(ms[0], ms[1]), ms[2])
    es = [jnp.exp(m - top) for m in ms]
    inv = 1.0 / (es[0] * ls[0] + es[1] * ls[1] + es[2] * ls[2])
    ex = ex_ref[...]
    nq = B_Q_DIM // LANES
    acc = None
    for (_, dil), e, a_ref in zip(B_GROUPS, es, a_refs):
        w = e * inv
        w_hi = w.astype(BF16)
        w_lo = (w - w_hi.astype(F32)).astype(BF16)
        wide = (jnp.dot(w_hi, ex, preferred_element_type=F32)
                + jnp.dot(w_lo, ex, preferred_element_type=F32))
        if dil == 1:
            part = a_ref[0, 0].astype(F32)
        else:
            rows = tm // dil
            for r in range(dil):
                val = a_ref[0, r].astype(F32)
                for c in range(nq):
                    nat_scr[c, pl.ds(r, rows, stride=dil), :] = val[:, c * LANES:(c + 1) * LANES]
            part = jnp.concatenate([nat_scr[c] for c in range(nq)], axis=1)
        term = wide * part
        acc = term if acc is None else acc + term
    return acc


def _post_ffn_kernel(x_ref, *refs, mixed, tm):
    if mixed:
        a_refs, (st_ref, ex_ref, wo_ref, g_ref, wi_ref, w2_ref, out_ref, acc_ref, nat_scr) = refs[:3], refs[3:]
        y_in = _mix_groups(a_refs, st_ref, ex_ref, nat_scr, tm).astype(BF16)
    else:
        y_ref, wo_ref, g_ref, wi_ref, w2_ref, out_ref, acc_ref = refs
        y_in = y_ref[...]
    y = jnp.dot(y_in, wo_ref[...], preferred_element_type=F32)
    x1 = x_ref[...] + _rms(y) * g_ref[0:1, :]
    out_ref[...] = x1
    hn = (_rms(x1) * g_ref[1:2, :]).astype(BF16)
    for j in range(0, D_FF, FF_CHUNK):
        gate = jnp.dot(hn, wi_ref[:, j:j + FF_CHUNK], preferred_element_type=F32)
        up = jnp.dot(hn, wi_ref[:, D_FF + j:D_FF + j + FF_CHUNK], preferred_element_type=F32)
        part = jnp.dot((_silu(gate) * up).astype(BF16), w2_ref[j:j + FF_CHUNK, :],
                       preferred_element_type=F32)
        if j == 0:
            acc_ref[...] = part
        else:
            acc_ref[...] += part
    out_ref[...] += _rms(acc_ref[...]) * g_ref[2:3, :]


def _post_ffn(x, y, w_o, gains, w_in, w_out, tm, seq=None):
    m, d = x.shape
    tm = min(tm, m)
    mixed = seq is not None
    row = lambda n: pl.BlockSpec((tm, n), lambda i: (i, 0))
    scratch = [pltpu.VMEM((tm, d), F32)]
    if mixed:
        accs, stats = y
        tiles = seq // tm
        ex = jnp.asarray(np.repeat(np.eye(B_HEADS, dtype=np.float32), B_DH, axis=1), BF16)
        y_args = [*accs, stats, ex]
        y_specs = [pl.BlockSpec((1, dil, tm // dil, B_Q_DIM), lambda i: (i // tiles, 0, i % tiles, 0))
                   for _, dil in B_GROUPS] + [row(stats.shape[1]), _resident(ex.shape)]
        scratch.append(pltpu.VMEM((B_Q_DIM // LANES, tm, LANES), F32))
    else:
        y_args, y_specs = [y], [row(y.shape[1])]
    return pl.pallas_call(
        functools.partial(_post_ffn_kernel, mixed=mixed, tm=tm),
        grid=(m // tm,),
        in_specs=[row(d)] + y_specs
        + [_resident(w_o.shape), _resident(gains.shape), _resident(w_in.shape), _resident(w_out.shape)],
        out_specs=row(d),
        out_shape=jax.ShapeDtypeStruct((m, d), F32),
        scratch_shapes=scratch,
        compiler_params=_params("parallel"),
        name="post_ffn",
    )(x, *y_args, w_o, gains, w_in, w_out)


def _decimate_rows(src_ref, n_chunks, dil, rows):
    return jnp.concatenate(
        [jnp.concatenate([src_ref[c, pl.ds(r, rows, stride=dil), :] for c in range(n_chunks)], axis=1)
         for r in range(dil)], axis=0)


def _b_proj_kernel(x_ref, g_ref, *refs, with_kv, ts):
    if with_kv:
        wk_ref, wv_ref, wq_ref, k_ref, v_ref = refs[:5]
        kd_refs, vd_refs, qd_refs = refs[5:8], refs[8:11], refs[11:14]
        xn_scr, kv_scr = refs[14:]
    else:
        wq_ref = refs[0]
        qd_refs = refs[1:4]
        (xn_scr,) = refs[4:]
    xs = _rms(x_ref[...])
    q_gain = 1 if with_kv else 0
    xq = xs * g_ref[q_gain:q_gain + 1, :]
    nq = D_MODEL // LANES
    for c in range(nq):
        xn_scr[c] = xq[:, c * LANES:(c + 1) * LANES]
    for gi, (_, dil) in enumerate(B_GROUPS):
        rows = ts // dil
        lhs = (xq if dil == 1 else _decimate_rows(xn_scr, nq, dil, rows)).astype(BF16)
        for n0 in range(0, B_Q_DIM, 512):
            res = jnp.dot(lhs, wq_ref[:, gi * B_Q_DIM + n0:gi * B_Q_DIM + n0 + 512], preferred_element_type=F32)
            qd_refs[gi][0, :, :, n0:n0 + 512] = res.reshape(dil, rows, 512).astype(BF16)
    if with_kv:
        xkv = (xs * g_ref[0:1, :]).astype(BF16)
        nk = B_KV_DIM // LANES
        for w_ref, nat_ref, d_refs in ((wk_ref, k_ref, kd_refs), (wv_ref, v_ref, vd_refs)):
            val = jnp.dot(xkv, w_ref[...], preferred_element_type=F32)
            nat_ref[...] = val
            for c in range(nk):
                kv_scr[c] = val[:, c * LANES:(c + 1) * LANES]
            for gi, (_, dil) in enumerate(B_GROUPS):
                rows = ts // dil
                dec = val if dil == 1 else _decimate_rows(kv_scr, nk, dil, rows)
                d_refs[gi][0] = dec.reshape(dil, rows, B_KV_DIM).astype(BF16)


def _b_proj(x, gains, weights, b, s, with_kv, ts):
    m, d = x.shape
    tiles = s // ts
    dec = lambda n: [pl.BlockSpec((1, dil, ts // dil, n), lambda i: (i // tiles, 0, i % tiles, 0))
                     for _, dil in B_GROUPS]
    dec_shape = lambda n: [jax.ShapeDtypeStruct((b, dil, s // dil, n), BF16) for _, dil in B_GROUPS]
    nat = pl.BlockSpec((ts, B_KV_DIM), lambda i: (i, 0))
    out_specs, out_shape = dec(B_Q_DIM), dec_shape(B_Q_DIM)
    scratch = [pltpu.VMEM((D_MODEL // LANES, ts, LANES), F32)]
    if with_kv:
        out_specs = [nat, nat] + dec(B_KV_DIM) + dec(B_KV_DIM) + out_specs
        out_shape = [jax.ShapeDtypeStruct((m, B_KV_DIM), F32)] * 2 + dec_shape(B_KV_DIM) * 2 + out_shape
        scratch.append(pltpu.VMEM((B_KV_DIM // LANES, ts, LANES), F32))
    return pl.pallas_call(
        functools.partial(_b_proj_kernel, with_kv=with_kv, ts=ts),
        grid=(m // ts,),
        in_specs=[pl.BlockSpec((ts, d), lambda i: (i, 0)), _resident(gains.shape)]
        + [_resident(w.shape) for w in weights],
        out_specs=out_specs,
        out_shape=out_shape,
        scratch_shapes=scratch,
        compiler_params=_params("parallel"),
        name="attn_proj",
    )(x, gains, *weights)


def _alibi_slopes():
    n = N_GROUPS * B_HEADS
    return (2.0 ** (-8.0 * np.arange(1, n + 1) / n)).astype(np.float32).reshape(N_GROUPS, B_HEADS)


def _band_bias(gi):
    _, dil = B_GROUPS[gi]
    qi = np.arange(B_BLOCK)[None, :]
    ci = np.arange(B_BLOCK)[:, None]
    delta = np.where(ci > qi, B_BLOCK + qi - ci, qi - ci).astype(np.float32)
    rest = -_alibi_slopes()[gi][:, None, None] * (dil * delta)[None]
    first = np.where((ci > qi)[None], np.float32(NEG), rest)
    return np.stack([first, rest]).astype(np.float32)


def _band_attn_kernel(q_ref, kp_ref, kc_ref, vp_ref, vc_ref, bias_ref, o_ref, st_ref,
                      sp_scr, sc_scr, eu_scr, el_scr, *, far_bias, bb):
    blk = pl.program_id(2)
    first = blk == 0
    sel = jnp.minimum(blk, 1)
    ci = lax.broadcasted_iota(jnp.int32, (B_BLOCK, B_BLOCK), 0)
    qi = lax.broadcasted_iota(jnp.int32, (B_BLOCK, B_BLOCK), 1)
    upper = ci > qi
    diag = ci == qi
    nt = (((1,), (1,)), ((), ()))
    for bi in range(bb):
        for g in range(B_KV_HEADS):
            cols = slice(g * B_DH, (g + 1) * B_DH)
            slot = bi * B_KV_HEADS + g
            qg = jnp.concatenate([q_ref[bi, 0, :, (g * B_QPK + p) * B_DH:(g * B_QPK + p + 1) * B_DH]
                                  for p in range(B_QPK)], axis=0)
            sp_scr[slot] = lax.dot_general(kp_ref[bi, 0, :, cols], qg, nt, preferred_element_type=F32)
            sc_scr[slot] = lax.dot_general(kc_ref[bi, 0, :, cols], qg, nt, preferred_element_type=F32)
    for bi in range(bb):
        for h in range(B_HEADS):
            g, p = divmod(h, B_QPK)
            slot = bi * B_KV_HEADS + g
            lanes = slice(p * B_BLOCK, (p + 1) * B_BLOCK)
            sp = sp_scr[slot, :, lanes]
            s = jnp.where(upper, sp, sc_scr[slot, :, lanes]) + bias_ref[sel, h]
            far = (jnp.sum(jnp.where(diag, sp, 0.0), axis=0, keepdims=True)
                   + jnp.where(first, NEG, far_bias[h]))
            mx = jnp.maximum(jnp.max(s, axis=0, keepdims=True), far)
            e = jnp.exp(s - mx)
            e_far = jnp.exp(far - mx)
            st_ref[bi, 0, 0, h:h + 1, :] = mx
            st_ref[bi, 0, 0, B_HEADS + h:B_HEADS + h + 1, :] = jnp.sum(e, axis=0, keepdims=True) + e_far
            eu_scr[slot, :, lanes] = jnp.where(upper, e, jnp.where(diag, e_far, 0.0)).astype(BF16)
            el_scr[slot, :, lanes] = jnp.where(upper, 0.0, e).astype(BF16)
    for bi in range(bb):
        for g in range(B_KV_HEADS):
            cols = slice(g * B_DH, (g + 1) * B_DH)
            slot = bi * B_KV_HEADS + g
            pv_t = (jnp.dot(vp_ref[bi, 0, :, cols].T, eu_scr[slot], preferred_element_type=F32)
                    + jnp.dot(vc_ref[bi, 0, :, cols].T, el_scr[slot], preferred_element_type=F32))
            pv = pv_t.T
            for p in range(B_QPK):
                h = g * B_QPK + p
                o_ref[bi, 0, :, h * B_DH:(h + 1) * B_DH] = pv[p * B_BLOCK:(p + 1) * B_BLOCK].astype(o_ref.dtype)


def _band_attn(qd, kd, vd, gi, bb):
    win, dil = B_GROUPS[gi]
    assert win // dil == B_BLOCK
    b, _, n_dec, _ = qd.shape
    nb = n_dec // B_BLOCK
    bias = jnp.asarray(_band_bias(gi))
    far_bias = tuple(float(-sl * win) for sl in _alibi_slopes()[gi])
    cur = lambda bi, r, j: (bi, r, j, 0)
    prev = lambda bi, r, j: (bi, r, jnp.maximum(j - 1, 0), 0)
    bb = math.gcd(bb, b)
    blk = lambda n, imap: pl.BlockSpec((bb, 1, B_BLOCK, n), imap)
    tiles = (bb * B_KV_HEADS, B_BLOCK, B_QPK * B_BLOCK)
    return pl.pallas_call(
        functools.partial(_band_attn_kernel, far_bias=far_bias, bb=bb),
        grid=(b // bb, dil, nb),
        in_specs=[blk(B_Q_DIM, cur), blk(B_KV_DIM, prev), blk(B_KV_DIM, cur), blk(B_KV_DIM, prev),
                  blk(B_KV_DIM, cur), _resident(bias.shape)],
        out_specs=[blk(B_Q_DIM, cur),
                   pl.BlockSpec((bb, 1, 1, 2 * B_HEADS, B_BLOCK), lambda bi, r, j: (bi, r, j, 0, 0))],
        out_shape=[jax.ShapeDtypeStruct((b, dil, n_dec, B_Q_DIM), BF16),
                   jax.ShapeDtypeStruct((b, dil, nb, 2 * B_HEADS, B_BLOCK), F32)],
        scratch_shapes=[pltpu.VMEM(tiles, F32), pltpu.VMEM(tiles, F32),
                        pltpu.VMEM(tiles, BF16), pltpu.VMEM(tiles, BF16)],
        compiler_params=_params("parallel", "parallel", "arbitrary"),
        name=f"band_attn_g{gi}",
    )(qd, kd, kd, vd, vd, bias)


def _decode_ranges(t_past):
    return tuple(max(0, t_past - win) // 16 * 16 for win, _ in B_GROUPS)


def _decode_bias(l_new, t_past):
    slopes = _alibi_slopes()
    rows_grp = B_KV_HEADS * B_QPK * l_new
    starts = _decode_ranges(t_past)
    kept = np.arange(t_past)
    bias_c = [np.full((rows_grp, t_past - st), NEG, np.float32) for st in starts]
    bias_n = np.full((N_GROUPS * rows_grp, l_new), NEG, np.float32)
    for gi, (win, dil) in enumerate(B_GROUPS):
        r = 0
        for g in range(B_KV_HEADS):
            for p in range(B_QPK):
                for l in range(l_new):
                    for pos, tbl, row in ((kept[starts[gi]:], bias_c[gi], r),
                                          (t_past + np.arange(l_new), bias_n, gi * rows_grp + r)):
                        dist = t_past + l - pos
                        ok = (dist >= 0) & (dist <= win) & (dist % dil == 0)
                        tbl[row] = np.where(ok, -slopes[gi, g * B_QPK + p] * dist.astype(np.float32), NEG)
                    r += 1
    return bias_c, bias_n


def _decode_attn_kernel(q_ref, ck_ref, cv_ref, kn_ref, vn_ref, b0_ref, b1_ref, b2_ref, bn_ref,
                        o_ref, *, l_new, bb, starts):
    rows_g = B_QPK * l_new
    rows_grp = B_KV_HEADS * rows_g
    lane = lax.broadcasted_iota(jnp.int32, (rows_g, B_KV_DIM), 1)
    keep = [(lane >= g * B_DH) & (lane < (g + 1) * B_DH) for g in range(B_KV_HEADS)]
    nt = (((1,), (1,)), ((), ()))
    for bi in range(bb):
        qt = q_ref[bi]
        ms, dens, accs = [], [], []
        for gi, b_ref in enumerate((b0_ref, b1_ref, b2_ref)):
            qg = qt[gi * rows_g:(gi + 1) * rows_g]
            qbd = jnp.concatenate([jnp.where(keep[g], qg, 0.0) for g in range(B_KV_HEADS)], axis=0)
            st = starts[gi]
            parts = [(ck_ref[bi, st:, :], cv_ref[bi, st:, :], b_ref[...]),
                     (kn_ref[bi], vn_ref[bi], bn_ref[gi * rows_grp:(gi + 1) * rows_grp, :])]
            scores = [lax.dot_general(qbd, k, nt, preferred_element_type=F32) + bias for k, _, bias in parts]
            mx = functools.reduce(jnp.maximum, [jnp.max(s, axis=-1, keepdims=True) for s in scores])
            es = [jnp.exp(s - mx) for s in scores]
            ms.append(mx)
            dens.append(sum(jnp.sum(e, axis=-1, keepdims=True) for e in es))
            accs.append(sum(jnp.dot(e, v, preferred_element_type=F32)
                            for e, (_, v, _) in zip(es, parts)))
        top = jnp.maximum(jnp.maximum(ms[0], ms[1]), ms[2])
        num, tot = None, None
        for i in range(N_GROUPS):
            a = jnp.exp(ms[i] - top)
            num = a * accs[i] if num is None else num + a * accs[i]
            tot = a * dens[i] if tot is None else tot + a * dens[i]
        out = num / tot
        o_ref[bi] = jnp.concatenate([out[g * rows_g:(g + 1) * rows_g, g * B_DH:(g + 1) * B_DH]
                                     for g in range(B_KV_HEADS)], axis=0)


def _decode_attn(q, k_new, v_new, cache_k, cache_v, b, l_new, bb):
    t_past = cache_k.shape[1]
    rows_g = B_QPK * l_new
    qt = q.reshape(b, l_new, N_GROUPS, B_KV_HEADS, B_QPK, B_DH)
    qt = jnp.transpose(qt, (0, 2, 4, 1, 3, 5)).reshape(b, N_GROUPS * rows_g, B_KV_DIM)
    bias_c, bias_n = _decode_bias(l_new, t_past)
    tables = [jnp.asarray(a) for a in (*bias_c, bias_n)]
    seq = lambda n, d: pl.BlockSpec((bb, n, d), lambda i: (i, 0, 0))
    out = pl.pallas_call(
        functools.partial(_decode_attn_kernel, l_new=l_new, bb=bb, starts=_decode_ranges(t_past)),
        grid=(b // bb,),
        in_specs=[seq(N_GROUPS * rows_g, B_KV_DIM), seq(t_past, B_KV_DIM), seq(t_past, B_KV_DIM),
                  seq(l_new, B_KV_DIM), seq(l_new, B_KV_DIM)] + [_resident(a.shape) for a in tables],
        out_specs=seq(B_KV_HEADS * rows_g, B_DH),
        out_shape=jax.ShapeDtypeStruct((b, B_KV_HEADS * rows_g, B_DH), F32),
        compiler_params=_params("parallel"),
        name="decode_attn",
    )(qt, cache_k, cache_v, k_new.reshape(b, l_new, B_KV_DIM), v_new.reshape(b, l_new, B_KV_DIM), *tables)
    out = out.reshape(b, B_HEADS, l_new, B_DH)
    return jnp.transpose(out, (0, 2, 1, 3)).reshape(b * l_new, B_Q_DIM).astype(BF16)


def _stats_token_order(st):
    b, dil, nb, n, blk = st.shape
    return jnp.transpose(st, (0, 2, 4, 1, 3)).reshape(b * dil * nb * blk, n)


def _trunk(x3, conv_hist, delta_s, kv_past, w, *, tm_proj, tm_ffn, gdn_bt, gdn_chunk, gdn_mm):
    b, t, d = x3.shape
    m = b * t
    x = x3.reshape(m, d)
    new_hist, new_delta = [], []
    k_new = v_new = kd = vd = None
    for layer in range(DEPTH):
        if layer < N_A_LAYERS:
            pm, bd = _norm_proj(x, w["norms"][layer, 0:1], [w["a_w_main"][layer], w["a_w_gate"][layer]],
                                (0, 0), ((0, F32), (1, F32)), tm_proj)
            y, s_new, h_new = _gdn(pm.reshape(b, t, A_MAIN_DIM), bd.reshape(b, t, GATE_PAD),
                                   conv_hist, delta_s, layer, w["a_conv_w"][layer], w["a_ab"][layer],
                                   w["a_o_gain"][layer], gdn_bt, gdn_chunk, gdn_mm)
            y = y.reshape(m, A_V_DIM)
            new_hist.append(h_new)
            new_delta.append(s_new)
            w_mix = w["a_w_out"][layer]
        else:
            j = layer - N_A_LAYERS
            q_gain = w["norms"][layer, 0]
            if kv_past is None:
                if j == 0:
                    res = _b_proj(x, jnp.stack([w["kv_norm"], q_gain]), [w["b_w_k"], w["b_w_v"], w["b_w_q"][j]],
                                  b, t, True, tm_ffn)
                    k_new, v_new, kd, vd, qd = res[0], res[1], res[2:5], res[5:8], res[8:11]
                else:
                    qd = _b_proj(x, q_gain[None], [w["b_w_q"][j]], b, t, False, tm_ffn)
                parts = [_band_attn(qd[gi], kd[gi], vd[gi], gi, 2) for gi in range(N_GROUPS)]
                stats = jnp.concatenate([_stats_token_order(p[1]) for p in parts], axis=1)
                y = ([p[0] for p in parts], stats)
            else:
                if j == 0:
                    k_new, v_new, q = _norm_proj(
                        x, jnp.stack([w["kv_norm"], q_gain]), [w["b_w_k"], w["b_w_v"], w["b_w_q"][j]],
                        (0, 0, 1), ((0, F32), (1, F32), (2, F32)), tm_proj)
                else:
                    (q,) = _norm_proj(x, q_gain[None], [w["b_w_q"][j]], (0,), ((0, F32),), tm_proj)
                y = _decode_attn(q, k_new, v_new, *kv_past, b, t, 2)
            w_mix = w["b_w_o"][j]
        x = _post_ffn(x, y, w_mix, w["norms"][layer, 1:4], w["ffn_w_in"][layer], w["ffn_w_out"][layer], tm_ffn,
                      seq=t if isinstance(y, tuple) else None)
    return (x.reshape(b, t, d), jnp.stack(new_hist), jnp.stack(new_delta),
            k_new.reshape(b, t, B_KV_HEADS, B_DH), v_new.reshape(b, t, B_KV_HEADS, B_DH))


def kernel(x_prompt, x_sample, state_conv, state_delta, cache_k, cache_v, norms, kv_norm, a_w_in,
           a_conv_w, a_log, a_dt_bias, a_o_gain, a_w_out, b_w_kv, b_w_q, b_w_o, ffn_w_in, ffn_w_out):
    bp, sp, _ = x_prompt.shape
    gate_w = a_w_in[:, :, A_MAIN_DIM:]
    gate_w = jnp.pad(gate_w, ((0, 0), (0, 0), (0, GATE_PAD - gate_w.shape[-1])))
    ab = jnp.pad(jnp.stack([a_log, a_dt_bias], axis=1), ((0, 0), (0, 0), (A_HEADS, GATE_PAD - 2 * A_HEADS)))
    w_kv = b_w_kv.reshape(D_MODEL, 2, B_KV_DIM)
    w = {
        "norms": norms, "kv_norm": kv_norm,
        "a_w_main": a_w_in[:, :, :A_MAIN_DIM].astype(BF16), "a_w_gate": gate_w.astype(BF16),
        "a_conv_w": a_conv_w, "a_ab": ab, "a_o_gain": a_o_gain.reshape(N_A_LAYERS, 1, A_DV),
        "a_w_out": a_w_out.astype(BF16),
        "b_w_k": w_kv[:, 0].astype(BF16), "b_w_v": w_kv[:, 1].astype(BF16),
        "b_w_q": (b_w_q * (B_DH ** -0.5)).astype(BF16), "b_w_o": b_w_o.astype(BF16),
        "ffn_w_in": ffn_w_in.astype(BF16), "ffn_w_out": ffn_w_out.astype(BF16),
    }
    zero_hist = jnp.zeros((N_A_LAYERS, bp, CONV_W - 1, A_CONV_DIM), x_prompt.dtype)
    zero_delta = jnp.zeros((N_A_LAYERS, bp, A_HEADS, A_DK, A_DV), x_prompt.dtype)
    y_p, conv_p, delta_p, k_p, v_p = _trunk(
        x_prompt, zero_hist, zero_delta, None, w,
        tm_proj=256, tm_ffn=512, gdn_bt=2, gdn_chunk=128, gdn_mm=BF16)
    bs, t_past = cache_k.shape[:2]
    kv_past = tuple(c.reshape(bs, t_past, B_KV_DIM) for c in (cache_k, cache_v))
    y_s, conv_s, delta_s, k_s, v_s = _trunk(
        x_sample, state_conv, state_delta, kv_past, w,
        tm_proj=256, tm_ffn=512, gdn_bt=8, gdn_chunk=x_sample.shape[1], gdn_mm=F32)
    win_p = min(MAX_WINDOW, sp)
    return (y_p, y_s, conv_p, delta_p, k_p[:, sp - win_p:], v_p[:, sp - win_p:], conv_s, delta_s, k_s, v_s)
```

```python
import functools
import math

import jax
import jax.numpy as jnp
import numpy as np
from jax import lax
from jax.experimental import pallas as pl
from jax.experimental.pallas import tpu as pltpu

F32 = jnp.float32
BF16 = jnp.bfloat16

D_MODEL = 1024
DEPTH = 4
N_A_LAYERS = 2
EPS = 1e-6

A_HEADS = 8
A_DK = 128
A_DV = 128
CONV_W = 4
A_QK_DIM = A_HEADS * A_DK
A_V_DIM = A_HEADS * A_DV
A_CONV_DIM = 2 * A_QK_DIM + A_V_DIM
A_MAIN_DIM = A_CONV_DIM + A_V_DIM
GATE_PAD = 128

B_GROUPS = ((128, 1), (512, 4), (2048, 16))
N_GROUPS = 3
B_HEADS = 16
B_DH = 64
B_KV_HEADS = 4
B_QPK = B_HEADS // B_KV_HEADS
B_BLOCK = 128
B_KV_DIM = B_KV_HEADS * B_DH
B_Q_DIM = B_HEADS * B_DH
MAX_WINDOW = 2048

D_FF = 2816
FF_CHUNK = 256

LANES = 128
NEG = -1e30
V7X_VMEM_LIMIT = 56 * 1024 * 1024
HIGHEST = lax.Precision.HIGHEST


def _params(*sem):
    return pltpu.CompilerParams(dimension_semantics=sem, vmem_limit_bytes=V7X_VMEM_LIMIT)


def _resident(shape):
    nd = len(shape)
    return pl.BlockSpec(shape, lambda *_: (0,) * nd, pipeline_mode=pl.Buffered(1))


def _rms(x):
    return x * lax.rsqrt(jnp.mean(x * x, axis=-1, keepdims=True) + EPS)


def _silu(x):
    return x * jax.nn.sigmoid(x)


def _norm_proj_kernel(x_ref, g_ref, *refs, group_of, outs, n_chunk):
    nw = len(group_of)
    w_refs, o_refs = refs[:nw], refs[nw:]
    xs = _rms(x_ref[...])
    normed = {}
    for wi, grp in enumerate(group_of):
        if grp not in normed:
            normed[grp] = (xs * g_ref[grp:grp + 1, :]).astype(BF16)
        xn = normed[grp]
        n = w_refs[wi].shape[1]
        for n0 in range(0, n, n_chunk):
            n1 = min(n0 + n_chunk, n)
            res = jnp.dot(xn, w_refs[wi][:, n0:n1], preferred_element_type=F32)
            for oi, (src, _) in enumerate(outs):
                if src == wi:
                    o_refs[oi][:, n0:n1] = res.astype(o_refs[oi].dtype)


def _norm_proj(x, gains, weights, group_of, outs, tm):
    m, d = x.shape
    tm = min(tm, m)
    kern = functools.partial(_norm_proj_kernel, group_of=tuple(group_of), outs=tuple(outs), n_chunk=512)
    return pl.pallas_call(
        kern,
        grid=(m // tm,),
        in_specs=[pl.BlockSpec((tm, d), lambda i: (i, 0)), _resident(gains.shape)]
        + [_resident(w.shape) for w in weights],
        out_specs=[pl.BlockSpec((tm, weights[src].shape[1]), lambda i: (i, 0)) for src, _ in outs],
        out_shape=[jax.ShapeDtypeStruct((m, weights[src].shape[1]), dt) for src, dt in outs],
        compiler_params=_params("parallel"),
        name="norm_proj",
    )(x, gains, *weights)


def _gdn_kernel(*refs, bt, chunk, mm_dtype, stacked):
    if stacked:
        pm_ref, bd_ref, hist_ref, s0_ref, cw_ref, ab_ref, og_ref, prev_ref, o_ref, out_ref, hnew_ref, xh_ref = refs
        s_ref = out_ref.at[1]
    else:
        pm_ref, bd_ref, hist_ref, s0_ref, cw_ref, ab_ref, og_ref, o_ref, s_ref, hnew_ref, xh_ref = refs
    nh, c = A_HEADS, chunk
    bh = bt * nh

    @pl.when(pl.program_id(1) == 0)
    def _start():
        s_ref[...] = s0_ref[...]
        xh_ref[:, 5:8, :] = hist_ref[...]
        if stacked:
            out_ref[0] = prev_ref[...]

    x = pm_ref[:, :, 0:A_CONV_DIM]
    xh_ref[:, 8:8 + c, :] = x
    cw = cw_ref[...]
    conv = x * cw[CONV_W - 1:CONV_W, :]
    for j in range(1, CONV_W):
        conv = conv + xh_ref[:, 8 - j:8 - j + c, :] * cw[CONV_W - 1 - j:CONV_W - j, :]
    tail = xh_ref[:, 5 + c:8 + c, :]
    xh_ref[:, 5:8, :] = tail
    hnew_ref[...] = tail
    act = _silu(conv)

    def heads(base):
        return jnp.stack([act[bi, :, base + h * A_DK: base + (h + 1) * A_DK]
                          for bi in range(bt) for h in range(nh)])

    qr, kr, v = heads(0), heads(A_QK_DIM), heads(2 * A_QK_DIM)
    q = qr * (lax.rsqrt(jnp.sum(qr * qr, axis=-1, keepdims=True) + EPS) * (A_DK ** -0.5))
    k = kr * lax.rsqrt(jnp.sum(kr * kr, axis=-1, keepdims=True) + EPS)

    bd = bd_ref[...]
    beta_all = jax.nn.sigmoid(bd)
    z = bd + ab_ref[1:2, :]
    softplus = jnp.maximum(z, 0.0) + jnp.log1p(jnp.exp(-jnp.abs(z)))
    g_all = -jnp.exp(ab_ref[0:1, :]) * softplus

    ti = lax.broadcasted_iota(jnp.int32, (c, c), 0)
    si = lax.broadcasted_iota(jnp.int32, (c, c), 1)
    incl = ti >= si
    strict = ti > si
    tri = incl.astype(F32)
    gcol, grow, bcol, glast = [], [], [], []
    for bi in range(bt):
        gc = jnp.dot(tri, g_all[bi], precision=HIGHEST, preferred_element_type=F32)
        gct = gc.T
        for h in range(nh):
            gcol.append(gc[:, nh + h:nh + h + 1])
            grow.append(gct[nh + h:nh + h + 1, :])
            glast.append(gc[c - 1:c, nh + h:nh + h + 1])
            bcol.append(beta_all[bi][:, h:h + 1])
    gcol, grow, bcol, glast = jnp.stack(gcol), jnp.stack(grow), jnp.stack(bcol), jnp.stack(glast)

    decay = jnp.exp(jnp.where(incl, gcol - grow, NEG))
    gam = jnp.exp(gcol)

    def bmm(a, b):
        return jnp.einsum("bij,bjk->bik", a.astype(mm_dtype), b.astype(mm_dtype),
                          preferred_element_type=F32)

    def bmm_nt(a, b):
        return jnp.einsum("bid,bjd->bij", a.astype(mm_dtype), b.astype(mm_dtype),
                          preferred_element_type=F32)

    kk = bmm_nt(k, k)
    qk = bmm_nt(q, k)
    def bmm_split(a, b):
        a_hi = a.astype(BF16)
        a_lo = (a - a_hi.astype(F32)).astype(BF16)
        b_hi = b.astype(BF16)
        b_lo = (b - b_hi.astype(F32)).astype(BF16)
        return bmm(a_hi, b_hi) + bmm(a_hi, b_lo) + bmm(a_lo, b_hi)

    n_mat = jnp.where(strict, -(bcol * decay * kk), 0.0)
    m_pow = q_inv = n_mat
    for _ in range(int(math.log2(c)) - 1):
        m_pow = bmm(m_pow, m_pow)
        q_inv = q_inv + m_pow + bmm(q_inv, m_pow)
    if mm_dtype == BF16:
        resid = n_mat - q_inv + bmm_split(n_mat, q_inv)
        q_inv = q_inv + resid + bmm(q_inv, resid)
    rhs = jnp.concatenate([(bcol * gam) * k, bcol * v], axis=-1)
    sol = rhs + bmm(q_inv, rhs)
    w_mat, u_base = sol[..., :A_DK], sol[..., A_DK:]

    s_old = s_ref[...].reshape(bh, A_DK, A_DV)
    u = u_base - bmm(w_mat, s_old)
    o = bmm(gam * q, s_old) + bmm(decay * qk, u)
    k_dec = jnp.exp(glast - gcol) * k
    s_new = jnp.exp(glast) * s_old + bmm(jnp.swapaxes(k_dec, 1, 2), u)
    s_ref[...] = s_new.reshape(bt, nh, A_DK, A_DV)

    on = _rms(o) * og_ref[...]
    for bi in range(bt):
        for h in range(nh):
            gate = pm_ref[bi, :, A_CONV_DIM + h * A_DV:A_CONV_DIM + (h + 1) * A_DV]
            o_ref[bi, :, h * A_DV:(h + 1) * A_DV] = (on[bi * nh + h] * _silu(gate)).astype(o_ref.dtype)


def _gdn(pm, bd, hist, s0, layer, conv_w, ab, o_gain, bt, chunk, mm_dtype, prev_state=None):
    b, t, _ = pm.shape
    bt = math.gcd(bt, b)
    stacked = prev_state is not None
    state_blk = (bt, A_HEADS, A_DK, A_DV)
    state_in = pl.BlockSpec(state_blk, lambda i, j: (i, 0, 0, 0))
    if stacked:
        state_out = pl.BlockSpec((2,) + state_blk, lambda i, j: (0, i, 0, 0, 0))
        state_shape = jax.ShapeDtypeStruct((2, b) + state_blk[1:], F32)
    else:
        state_out, state_shape = state_in, jax.ShapeDtypeStruct((b,) + state_blk[1:], F32)
    return pl.pallas_call(
        functools.partial(_gdn_kernel, bt=bt, chunk=chunk, mm_dtype=mm_dtype, stacked=stacked),
        grid=(b // bt, t // chunk),
        in_specs=[
            pl.BlockSpec((bt, chunk, A_MAIN_DIM), lambda i, j: (i, j, 0)),
            pl.BlockSpec((bt, chunk, GATE_PAD), lambda i, j: (i, j, 0)),
            pl.BlockSpec((None, bt, CONV_W - 1, A_CONV_DIM), lambda i, j: (layer, i, 0, 0)),
            pl.BlockSpec((None,) + state_blk, lambda i, j: (layer, i, 0, 0, 0)),
            pl.BlockSpec((CONV_W, A_CONV_DIM), lambda i, j: (0, 0)),
            pl.BlockSpec((2, GATE_PAD), lambda i, j: (0, 0)),
            pl.BlockSpec((1, A_DV), lambda i, j: (0, 0)),
        ] + ([state_in] if stacked else []),
        out_specs=[
            pl.BlockSpec((bt, chunk, A_V_DIM), lambda i, j: (i, j, 0)),
            state_out,
            pl.BlockSpec((bt, CONV_W - 1, A_CONV_DIM), lambda i, j: (i, 0, 0)),
        ],
        out_shape=[
            jax.ShapeDtypeStruct((b, t, A_V_DIM), BF16),
            state_shape,
            jax.ShapeDtypeStruct((b, CONV_W - 1, A_CONV_DIM), F32),
        ],
        scratch_shapes=[pltpu.VMEM((bt, chunk + 8, A_CONV_DIM), F32)],
        compiler_params=_params("parallel", "arbitrary"),
        name="gdn_mixer",
    )(pm, bd, hist, s0, conv_w, ab, o_gain, *([prev_state] if stacked else []))


def _mix_groups(a_refs, st_ref, ex_ref, nat_scr, tm):
    st = st_ref[...]
    ms = [st[:, g * 2 * B_HEADS:g * 2 * B_HEADS + B_HEADS] for g in range(N_GROUPS)]
    ls = [st[:, g * 2 * B_HEADS + B_HEADS:(g + 1) * 2 * B_HEADS] for g in range(N_GROUPS)]
    top = jnp.maximum(jnp.maximum(ms[0], ms[1]), ms[2])
    es = [jnp.exp(m - top) for m in ms]
    inv = 1.0 / (es[0] * ls[0] + es[1] * ls[1] + es[2] * ls[2])
    ex = ex_ref[...]
    nq = B_Q_DIM // LANES
    acc = None
    for (_, dil), e, a_ref in zip(B_GROUPS, es, a_refs):
        w = e * inv
        w_hi = w.astype(BF16)
        w_lo = (w - w_hi.astype(F32)).astype(BF16)
        wide = (jnp.dot(w_hi, ex, preferred_element_type=F32)
                + jnp.dot(w_lo, ex, preferred_element_type=F32))
        if dil == 1:
            part = a_ref[0, 0].astype(F32)
        else:
            rows = tm // dil
            for r in range(dil):
                val = a_ref[0, r].astype(F32)
                for c in range(nq):
                    nat_scr[c, pl.ds(r, rows, stride=dil), :] = val[:, c * LANES:(c + 1) * LANES]
            part = jnp.concatenate([nat_scr[c] for c in range(nq)], axis=1)
        term = wide * part
        acc = term if acc is None else acc + term
    return acc


def _post_ffn_kernel(x_ref, *refs, mixed, tm):
    if mixed:
        a_refs, (st_ref, ex_ref, wo_ref, g_ref, wi_ref, w2_ref, out_ref, acc_ref, nat_scr) = refs[:3], refs[3:]
        y_in = _mix_groups(a_refs, st_ref, ex_ref, nat_scr, tm).astype(BF16)
    else:
        y_ref, wo_ref, g_ref, wi_ref, w2_ref, out_ref, acc_ref = refs
        y_in = y_ref[...]
    y = jnp.dot(y_in, wo_ref[...], preferred_element_type=F32)
    x1 = x_ref[...] + _rms(y) * g_ref[0:1, :]
    out_ref[...] = x1
    hn = (_rms(x1) * g_ref[1:2, :]).astype(BF16)
    for j in range(0, D_FF, FF_CHUNK):
        gate = jnp.dot(hn, wi_ref[:, j:j + FF_CHUNK], preferred_element_type=F32)
        up = jnp.dot(hn, wi_ref[:, D_FF + j:D_FF + j + FF_CHUNK], preferred_element_type=F32)
        part = jnp.dot((_silu(gate) * up).astype(BF16), w2_ref[j:j + FF_CHUNK, :],
                       preferred_element_type=F32)
        if j == 0:
            acc_ref[...] = part
        else:
            acc_ref[...] += part
    out_ref[...] += _rms(acc_ref[...]) * g_ref[2:3, :]


def _post_ffn(x, y, w_o, gains, w_in, w_out, tm, seq=None):
    m, d = x.shape
    tm = min(tm, m)
    mixed = seq is not None
    row = lambda n: pl.BlockSpec((tm, n), lambda i: (i, 0))
    scratch = [pltpu.VMEM((tm, d), F32)]
    if mixed:
        accs, stats = y
        tiles = seq // tm
        ex = jnp.asarray(np.repeat(np.eye(B_HEADS, dtype=np.float32), B_DH, axis=1), BF16)
        y_args = [*accs, stats, ex]
        y_specs = [pl.BlockSpec((1, dil, tm // dil, B_Q_DIM), lambda i: (i // tiles, 0, i % tiles, 0))
                   for _, dil in B_GROUPS] + [row(stats.shape[1]), _resident(ex.shape)]
        scratch.append(pltpu.VMEM((B_Q_DIM // LANES, tm, LANES), F32))
    else:
        y_args, y_specs = [y], [row(y.shape[1])]
    return pl.pallas_call(
        functools.partial(_post_ffn_kernel, mixed=mixed, tm=tm),
        grid=(m // tm,),
        in_specs=[row(d)] + y_specs
        + [_resident(w_o.shape), _resident(gains.shape), _resident(w_in.shape), _resident(w_out.shape)],
        out_specs=row(d),
        out_shape=jax.ShapeDtypeStruct((m, d), F32),
        scratch_shapes=scratch,
        compiler_params=_params("parallel"),
        name="post_ffn",
    )(x, *y_args, w_o, gains, w_in, w_out)


def _decimate_rows(src_ref, n_chunks, dil, rows):
    return jnp.concatenate(
        [jnp.concatenate([src_ref[c, pl.ds(r, rows, stride=dil), :] for c in range(n_chunks)], axis=1)
         for r in range(dil)], axis=0)


def _b_proj_kernel(x_ref, g_ref, *refs, with_kv, ts):
    if with_kv:
        wk_ref, wv_ref, wq_ref, k_ref, v_ref = refs[:5]
        kd_refs, vd_refs, qd_refs = refs[5:8], refs[8:11], refs[11:14]
        xn_scr, kv_scr = refs[14:]
    else:
        wq_ref = refs[0]
        qd_refs = refs[1:4]
        (xn_scr,) = refs[4:]
    xs = _rms(x_ref[...])
    q_gain = 1 if with_kv else 0
    xq = xs * g_ref[q_gain:q_gain + 1, :]
    nq = D_MODEL // LANES
    for c in range(nq):
        xn_scr[c] = xq[:, c * LANES:(c + 1) * LANES]
    for gi, (_, dil) in enumerate(B_GROUPS):
        rows = ts // dil
        lhs = (xq if dil == 1 else _decimate_rows(xn_scr, nq, dil, rows)).astype(BF16)
        for n0 in range(0, B_Q_DIM, 512):
            res = jnp.dot(lhs, wq_ref[:, gi * B_Q_DIM + n0:gi * B_Q_DIM + n0 + 512], preferred_element_type=F32)
            qd_refs[gi][0, :, :, n0:n0 + 512] = res.reshape(dil, rows, 512).astype(BF16)
    if with_kv:
        xkv = (xs * g_ref[0:1, :]).astype(BF16)
        nk = B_KV_DIM // LANES
        for w_ref, nat_ref, d_refs in ((wk_ref, k_ref, kd_refs), (wv_ref, v_ref, vd_refs)):
            val = jnp.dot(xkv, w_ref[...], preferred_element_type=F32)
            nat_ref[...] = val
            for c in range(nk):
                kv_scr[c] = val[:, c * LANES:(c + 1) * LANES]
            for gi, (_, dil) in enumerate(B_GROUPS):
                rows = ts // dil
                dec = val if dil == 1 else _decimate_rows(kv_scr, nk, dil, rows)
                d_refs[gi][0] = dec.reshape(dil, rows, B_KV_DIM).astype(BF16)


def _b_proj(x, gains, weights, b, s, with_kv, ts):
    m, d = x.shape
    tiles = s // ts
    dec = lambda n: [pl.BlockSpec((1, dil, ts // dil, n), lambda i: (i // tiles, 0, i % tiles, 0))
                     for _, dil in B_GROUPS]
    dec_shape = lambda n: [jax.ShapeDtypeStruct((b, dil, s // dil, n), BF16) for _, dil in B_GROUPS]
    nat = pl.BlockSpec((ts, B_KV_DIM), lambda i: (i, 0))
    out_specs, out_shape = dec(B_Q_DIM), dec_shape(B_Q_DIM)
    scratch = [pltpu.VMEM((D_MODEL // LANES, ts, LANES), F32)]
    if with_kv:
        out_specs = [nat, nat] + dec(B_KV_DIM) + dec(B_KV_DIM) + out_specs
        out_shape = [jax.ShapeDtypeStruct((m, B_KV_DIM), F32)] * 2 + dec_shape(B_KV_DIM) * 2 + out_shape
        scratch.append(pltpu.VMEM((B_KV_DIM // LANES, ts, LANES), F32))
    return pl.pallas_call(
        functools.partial(_b_proj_kernel, with_kv=with_kv, ts=ts),
        grid=(m // ts,),
        in_specs=[pl.BlockSpec((ts, d), lambda i: (i, 0)), _resident(gains.shape)]
        + [_resident(w.shape) for w in weights],
        out_specs=out_specs,
        out_shape=out_shape,
        scratch_shapes=scratch,
        compiler_params=_params("parallel"),
        name="attn_proj",
    )(x, gains, *weights)


def _alibi_slopes():
    n = N_GROUPS * B_HEADS
    return (2.0 ** (-8.0 * np.arange(1, n + 1) / n)).astype(np.float32).reshape(N_GROUPS, B_HEADS)


def _band_bias(gi):
    _, dil = B_GROUPS[gi]
    qi = np.arange(B_BLOCK)[None, :]
    ci = np.arange(B_BLOCK)[:, None]
    delta = np.where(ci > qi, B_BLOCK + qi - ci, qi - ci).astype(np.float32)
    rest = -_alibi_slopes()[gi][:, None, None] * (dil * delta)[None]
    first = np.where((ci > qi)[None], np.float32(NEG), rest)
    return np.stack([first, rest]).astype(np.float32)


def _band_attn_kernel(q_ref, kp_ref, kc_ref, vp_ref, vc_ref, bias_ref, o_ref, st_ref,
                      sp_scr, sc_scr, eu_scr, el_scr, *, far_bias, bb):
    blk = pl.program_id(2)
    first = blk == 0
    sel = jnp.minimum(blk, 1)
    ci = lax.broadcasted_iota(jnp.int32, (B_BLOCK, B_BLOCK), 0)
    qi = lax.broadcasted_iota(jnp.int32, (B_BLOCK, B_BLOCK), 1)
    upper = ci > qi
    diag = ci == qi
    nt = (((1,), (1,)), ((), ()))
    for bi in range(bb):
        for g in range(B_KV_HEADS):
            cols = slice(g * B_DH, (g + 1) * B_DH)
            slot = bi * B_KV_HEADS + g
            qg = jnp.concatenate([q_ref[bi, 0, :, (g * B_QPK + p) * B_DH:(g * B_QPK + p + 1) * B_DH]
                                  for p in range(B_QPK)], axis=0)
            sp_scr[slot] = lax.dot_general(kp_ref[bi, 0, :, cols], qg, nt, preferred_element_type=F32)
            sc_scr[slot] = lax.dot_general(kc_ref[bi, 0, :, cols], qg, nt, preferred_element_type=F32)
    for bi in range(bb):
        for h in range(B_HEADS):
            g, p = divmod(h, B_QPK)
            slot = bi * B_KV_HEADS + g
            lanes = slice(p * B_BLOCK, (p + 1) * B_BLOCK)
            sp = sp_scr[slot, :, lanes]
            s = jnp.where(upper, sp, sc_scr[slot, :, lanes]) + bias_ref[sel, h]
            far = (jnp.sum(jnp.where(diag, sp, 0.0), axis=0, keepdims=True)
                   + jnp.where(first, NEG, far_bias[h]))
            mx = jnp.maximum(jnp.max(s, axis=0, keepdims=True), far)
            e = jnp.exp(s - mx)
            e_far = jnp.exp(far - mx)
            st_ref[bi, 0, 0, h:h + 1, :] = mx
            st_ref[bi, 0, 0, B_HEADS + h:B_HEADS + h + 1, :] = jnp.sum(e, axis=0, keepdims=True) + e_far
            eu_scr[slot, :, lanes] = jnp.where(upper, e, jnp.where(diag, e_far, 0.0)).astype(BF16)
            el_scr[slot, :, lanes] = jnp.where(upper, 0.0, e).astype(BF16)
    for bi in range(bb):
        for g in range(B_KV_HEADS):
            cols = slice(g * B_DH, (g + 1) * B_DH)
            slot = bi * B_KV_HEADS + g
            pv_t = (jnp.dot(vp_ref[bi, 0, :, cols].T, eu_scr[slot], preferred_element_type=F32)
                    + jnp.dot(vc_ref[bi, 0, :, cols].T, el_scr[slot], preferred_element_type=F32))
            pv = pv_t.T
            for p in range(B_QPK):
                h = g * B_QPK + p
                o_ref[bi, 0, :, h * B_DH:(h + 1) * B_DH] = pv[p * B_BLOCK:(p + 1) * B_BLOCK].astype(o_ref.dtype)


def _band_attn(qd, kd, vd, gi, bb):
    win, dil = B_GROUPS[gi]
    assert win // dil == B_BLOCK
    b, _, n_dec, _ = qd.shape
    nb = n_dec // B_BLOCK
    bias = jnp.asarray(_band_bias(gi))
    far_bias = tuple(float(-sl * win) for sl in _alibi_slopes()[gi])
    cur = lambda bi, r, j: (bi, r, j, 0)
    prev = lambda bi, r, j: (bi, r, jnp.maximum(j - 1, 0), 0)
    bb = math.gcd(bb, b)
    blk = lambda n, imap: pl.BlockSpec((bb, 1, B_BLOCK, n), imap)
    tiles = (bb * B_KV_HEADS, B_BLOCK, B_QPK * B_BLOCK)
    return pl.pallas_call(
        functools.partial(_band_attn_kernel, far_bias=far_bias, bb=bb),
        grid=(b // bb, dil, nb),
        in_specs=[blk(B_Q_DIM, cur), blk(B_KV_DIM, prev), blk(B_KV_DIM, cur), blk(B_KV_DIM, prev),
                  blk(B_KV_DIM, cur), _resident(bias.shape)],
        out_specs=[blk(B_Q_DIM, cur),
                   pl.BlockSpec((bb, 1, 1, 2 * B_HEADS, B_BLOCK), lambda bi, r, j: (bi, r, j, 0, 0))],
        out_shape=[jax.ShapeDtypeStruct((b, dil, n_dec, B_Q_DIM), BF16),
                   jax.ShapeDtypeStruct((b, dil, nb, 2 * B_HEADS, B_BLOCK), F32)],
        scratch_shapes=[pltpu.VMEM(tiles, F32), pltpu.VMEM(tiles, F32),
                        pltpu.VMEM(tiles, BF16), pltpu.VMEM(tiles, BF16)],
        compiler_params=_params("parallel", "parallel", "arbitrary"),
        name=f"band_attn_g{gi}",
    )(qd, kd, kd, vd, vd, bias)


def _decode_ranges(t_past):
    return tuple(max(0, t_past - win) // 16 * 16 for win, _ in B_GROUPS)


def _decode_bias(l_new, t_past):
    slopes = _alibi_slopes()
    rows_grp = B_KV_HEADS * B_QPK * l_new
    starts = _decode_ranges(t_past)
    kept = np.arange(t_past)
    bias_c = [np.full((rows_grp, t_past - st), NEG, np.float32) for st in starts]
    bias_n = np.full((N_GROUPS * rows_grp, l_new), NEG, np.float32)
    for gi, (win, dil) in enumerate(B_GROUPS):
        r = 0
        for g in range(B_KV_HEADS):
            for p in range(B_QPK):
                for l in range(l_new):
                    for pos, tbl, row in ((kept[starts[gi]:], bias_c[gi], r),
                                          (t_past + np.arange(l_new), bias_n, gi * rows_grp + r)):
                        dist = t_past + l - pos
                        ok = (dist >= 0) & (dist <= win) & (dist % dil == 0)
                        tbl[row] = np.where(ok, -slopes[gi, g * B_QPK + p] * dist.astype(np.float32), NEG)
                    r += 1
    return bias_c, bias_n


def _decode_attn_kernel(q_ref, ck_ref, cv_ref, kn_ref, vn_ref, b0_ref, b1_ref, b2_ref, bn_ref,
                        o_ref, *, l_new, bb, starts):
    rows_g = B_QPK * l_new
    rows_grp = B_KV_HEADS * rows_g
    lane = lax.broadcasted_iota(jnp.int32, (rows_g, B_KV_DIM), 1)
    keep = [(lane >= g * B_DH) & (lane < (g + 1) * B_DH) for g in range(B_KV_HEADS)]
    nt = (((1,), (1,)), ((), ()))
    for bi in range(bb):
        qt = q_ref[bi]
        ms, dens, accs = [], [], []
        for gi, b_ref in enumerate((b0_ref, b1_ref, b2_ref)):
            qg = qt[gi * rows_g:(gi + 1) * rows_g]
            qbd = jnp.concatenate([jnp.where(keep[g], qg, 0.0) for g in range(B_KV_HEADS)], axis=0)
            st = starts[gi]
            parts = [(ck_ref[bi, st:, :], cv_ref[bi, st:, :], b_ref[...]),
                     (kn_ref[bi], vn_ref[bi], bn_ref[gi * rows_grp:(gi + 1) * rows_grp, :])]
            scores = [lax.dot_general(qbd, k, nt, preferred_element_type=F32) + bias for k, _, bias in parts]
            mx = functools.reduce(jnp.maximum, [jnp.max(s, axis=-1, keepdims=True) for s in scores])
            es = [jnp.exp(s - mx) for s in scores]
            ms.append(mx)
            dens.append(sum(jnp.sum(e, axis=-1, keepdims=True) for e in es))
            accs.append(sum(jnp.dot(e, v, preferred_element_type=F32)
                            for e, (_, v, _) in zip(es, parts)))
        top = jnp.maximum(jnp.maximum(ms[0], ms[1]), ms[2])
        num, tot = None, None
        for i in range(N_GROUPS):
            a = jnp.exp(ms[i] - top)
            num = a * accs[i] if num is None else num + a * accs[i]
            tot = a * dens[i] if tot is None else tot + a * dens[i]
        out = num / tot
        o_ref[bi] = jnp.concatenate([out[g * rows_g:(g + 1) * rows_g, g * B_DH:(g + 1) * B_DH]
                                     for g in range(B_KV_HEADS)], axis=0)


def _decode_attn(q, k_new, v_new, cache_k, cache_v, b, l_new, bb):
    t_past = cache_k.shape[1]
    rows_g = B_QPK * l_new
    qt = q.reshape(b, l_new, N_GROUPS, B_KV_HEADS, B_QPK, B_DH)
    qt = jnp.transpose(qt, (0, 2, 4, 1, 3, 5)).reshape(b, N_GROUPS * rows_g, B_KV_DIM)
    bias_c, bias_n = _decode_bias(l_new, t_past)
    tables = [jnp.asarray(a) for a in (*bias_c, bias_n)]
    seq = lambda n, d: pl.BlockSpec((bb, n, d), lambda i: (i, 0, 0))
    out = pl.pallas_call(
        functools.partial(_decode_attn_kernel, l_new=l_new, bb=bb, starts=_decode_ranges(t_past)),
        grid=(b // bb,),
        in_specs=[seq(N_GROUPS * rows_g, B_KV_DIM), seq(t_past, B_KV_DIM), seq(t_past, B_KV_DIM),
                  seq(l_new, B_KV_DIM), seq(l_new, B_KV_DIM)] + [_resident(a.shape) for a in tables],
        out_specs=seq(B_KV_HEADS * rows_g, B_DH),
        out_shape=jax.ShapeDtypeStruct((b, B_KV_HEADS * rows_g, B_DH), F32),
        compiler_params=_params("parallel"),
        name="decode_attn",
    )(qt, cache_k, cache_v, k_new.reshape(b, l_new, B_KV_DIM), v_new.reshape(b, l_new, B_KV_DIM), *tables)
    out = out.reshape(b, B_HEADS, l_new, B_DH)
    return jnp.transpose(out, (0, 2, 1, 3)).reshape(b * l_new, B_Q_DIM).astype(BF16)


def _stats_token_order(st):
    b, dil, nb, n, blk = st.shape
    return jnp.transpose(st, (0, 2, 4, 1, 3)).reshape(b * dil * nb * blk, n)


def _trunk(x3, conv_hist, delta_s, kv_past, w, *, tm_proj, tm_ffn, gdn_bt, gdn_chunk, gdn_mm):
    b, t, d = x3.shape
    m = b * t
    x = x3.reshape(m, d)
    assert N_A_LAYERS == 2
    new_hist, new_delta = [], None
    k_new = v_new = kd = vd = None
    for layer in range(DEPTH):
        if layer < N_A_LAYERS:
            pm, bd = _norm_proj(x, w["norms"][layer, 0:1], [w["a_w_main"][layer], w["a_w_gate"][layer]],
                                (0, 0), ((0, F32), (1, F32)), tm_proj)
            y, s_new, h_new = _gdn(pm.reshape(b, t, A_MAIN_DIM), bd.reshape(b, t, GATE_PAD),
                                   conv_hist, delta_s, layer, w["a_conv_w"][layer], w["a_ab"][layer],
                                   w["a_o_gain"][layer], gdn_bt, gdn_chunk, gdn_mm,
                                   prev_state=new_delta if layer == N_A_LAYERS - 1 else None)
            y = y.reshape(m, A_V_DIM)
            new_hist.append(h_new)
            new_delta = s_new
            w_mix = w["a_w_out"][layer]
        else:
            j = layer - N_A_LAYERS
            q_gain = w["norms"][layer, 0]
            if kv_past is None:
                if j == 0:
                    res = _b_proj(x, jnp.stack([w["kv_norm"], q_gain]), [w["b_w_k"], w["b_w_v"], w["b_w_q"][j]],
                                  b, t, True, tm_ffn)
                    k_new, v_new, kd, vd, qd = res[0], res[1], res[2:5], res[5:8], res[8:11]
                else:
                    qd = _b_proj(x, q_gain[None], [w["b_w_q"][j]], b, t, False, tm_ffn)
                parts = [_band_attn(qd[gi], kd[gi], vd[gi], gi, 2) for gi in range(N_GROUPS)]
                stats = jnp.concatenate([_stats_token_order(p[1]) for p in parts], axis=1)
                y = ([p[0] for p in parts], stats)
            else:
                if j == 0:
                    k_new, v_new, q = _norm_proj(
                        x, jnp.stack([w["kv_norm"], q_gain]), [w["b_w_k"], w["b_w_v"], w["b_w_q"][j]],
                        (0, 0, 1), ((0, F32), (1, F32), (2, F32)), tm_proj)
                else:
                    (q,) = _norm_proj(x, q_gain[None], [w["b_w_q"][j]], (0,), ((0, F32),), tm_proj)
                y = _decode_attn(q, k_new, v_new, *kv_past, b, t, 2)
            w_mix = w["b_w_o"][j]
        x = _post_ffn(x, y, w_mix, w["norms"][layer, 1:4], w["ffn_w_in"][layer], w["ffn_w_out"][layer], tm_ffn,
                      seq=t if isinstance(y, tuple) else None)
    return (x.reshape(b, t, d), jnp.stack(new_hist), new_delta,
            k_new.reshape(b, t, B_KV_HEADS, B_DH), v_new.reshape(b, t, B_KV_HEADS, B_DH))


def kernel(x_prompt, x_sample, state_conv, state_delta, cache_k, cache_v, norms, kv_norm, a_w_in,
           a_conv_w, a_log, a_dt_bias, a_o_gain, a_w_out, b_w_kv, b_w_q, b_w_o, ffn_w_in, ffn_w_out):
    bp, sp, _ = x_prompt.shape
    gate_w = a_w_in[:, :, A_MAIN_DIM:]
    gate_w = jnp.pad(gate_w, ((0, 0), (0, 0), (0, GATE_PAD - gate_w.shape[-1])))
    ab = jnp.pad(jnp.stack([a_log, a_dt_bias], axis=1), ((0, 0), (0, 0), (A_HEADS, GATE_PAD - 2 * A_HEADS)))
    w_kv = b_w_kv.reshape(D_MODEL, 2, B_KV_DIM)
    w = {
        "norms": norms, "kv_norm": kv_norm,
        "a_w_main": a_w_in[:, :, :A_MAIN_DIM].astype(BF16), "a_w_gate": gate_w.astype(BF16),
        "a_conv_w": a_conv_w, "a_ab": ab, "a_o_gain": a_o_gain.reshape(N_A_LAYERS, 1, A_DV),
        "a_w_out": a_w_out.astype(BF16),
        "b_w_k": w_kv[:, 0].astype(BF16), "b_w_v": w_kv[:, 1].astype(BF16),
        "b_w_q": (b_w_q * (B_DH ** -0.5)).astype(BF16), "b_w_o": b_w_o.astype(BF16),
        "ffn_w_in": ffn_w_in.astype(BF16), "ffn_w_out": ffn_w_out.astype(BF16),
    }
    zero_hist = jnp.zeros((N_A_LAYERS, bp, CONV_W - 1, A_CONV_DIM), x_prompt.dtype)
    zero_delta = jnp.zeros((N_A_LAYERS, bp, A_HEADS, A_DK, A_DV), x_prompt.dtype)
    y_p, conv_p, delta_p, k_p, v_p = _trunk(
        x_prompt, zero_hist, zero_delta, None, w,
        tm_proj=256, tm_ffn=512, gdn_bt=2, gdn_chunk=128, gdn_mm=BF16)
    bs, t_past = cache_k.shape[:2]
    kv_past = tuple(c.reshape(bs, t_past, B_KV_DIM) for c in (cache_k, cache_v))
    y_s, conv_s, delta_s, k_s, v_s = _trunk(
        x_sample, state_conv, state_delta, kv_past, w,
        tm_proj=256, tm_ffn=512, gdn_bt=8, gdn_chunk=x_sample.shape[1], gdn_mm=F32)
    win_p = min(MAX_WINDOW, sp)
    return (y_p, y_s, conv_p, delta_p, k_p[:, sp - win_p:], v_p[:, sp - win_p:], conv_s, delta_s, k_s, v_s)
```
